```python
import jax, jax.numpy as jnp
from jax import lax
import numpy as np

D_MODEL = 2048
BATCH = 4
SEQ = 4096
DEPTH = 1
DEC_BATCH = 4
DEC_SEQ = 2048
PAST_LEN = 128

N_HEADS = 4
D_K = D_MODEL // 2
D_V = D_MODEL
HEAD_DK = D_K // N_HEADS
HEAD_DV = D_V // N_HEADS
GATE_RANK = 16
GATE_TAU = 16.0
CHUNK = 64
CONV_DIM = D_MODEL
CONV_WIDTH = 31
CONV_PAD = (CONV_WIDTH - 1) // 2
D_FF = 4 * D_MODEL
EPS = 1e-6

SPLITS = [D_K, D_K, D_V, D_V, GATE_RANK, GATE_RANK, CONV_DIM, CONV_DIM, D_MODEL, D_MODEL]
SPLIT_IDX = list(np.cumsum(SPLITS)[:-1])
D_IN = int(sum(SPLITS))

kernel_name = "hybrid_gla_conformer_encoder"


def rms_norm(x, g):
    xf = x.astype(jnp.float32)
    y = xf * lax.rsqrt(jnp.mean(xf * xf, axis=-1, keepdims=True) + EPS)
    return (y * g.astype(jnp.float32)).astype(x.dtype)


def layer_norm(x, g, b):
    xf = x.astype(jnp.float32)
    mu = jnp.mean(xf, axis=-1, keepdims=True)
    xc = xf - mu
    var = jnp.mean(xc * xc, axis=-1, keepdims=True)
    y = xc * lax.rsqrt(var + EPS) * g.astype(jnp.float32) + b.astype(jnp.float32)
    return y.astype(x.dtype)


def gla_scan(q, k, v, log_a):
    B, H, S, dk = q.shape
    dv = v.shape[-1]
    n = S // CHUNK

    def to_chunks(t):
        return t.reshape(B, H, n, CHUNK, t.shape[-1]).transpose(2, 0, 1, 3, 4)

    qc, kc, vc = to_chunks(q), to_chunks(k), to_chunks(v)
    bc = jnp.cumsum(to_chunks(log_a), axis=-2)
    mask = jnp.tril(jnp.ones((CHUNK, CHUNK), dtype=bool))

    def step(state, inp):
        qi, ki, vi, bi = inp
        b_last = bi[..., -1:, :]
        b_ref = bi[..., CHUNK // 2:CHUNK // 2 + 1, :]
        o_inter = jnp.einsum('bhck,bhkv->bhcv', qi * jnp.exp(bi), state)
        scores = jnp.einsum('bhik,bhjk->bhij', qi * jnp.exp(bi - b_ref), ki * jnp.exp(b_ref - bi))
        scores = jnp.where(mask, scores, 0.0)
        o_intra = jnp.einsum('bhij,bhjv->bhiv', scores, vi)
        k_dec = ki * jnp.exp(b_last - bi)
        new_state = jnp.exp(b_last)[..., 0, :, None] * state + jnp.einsum('bhck,bhcv->bhkv', k_dec, vi)
        return new_state, o_inter + o_intra

    state0 = jnp.zeros((B, H, dk, dv), jnp.float32)
    _, o = lax.scan(step, state0, (qc, kc, vc, bc))
    return o.transpose(1, 2, 0, 3, 4).reshape(B, H, S, dv)


def encoder_layer(x, norm_mix_pre, norm_mix_post, norm_ffn_pre, norm_ffn_post, w_in,
                  w_a_fwd, b_a_fwd, w_a_bwd, b_a_bwd, gla_norm, conv_w, conv_b,
                  conv_ln_g, conv_ln_b, w_pw2, w_out, w_ff1, w_ff2):
    B, S, _ = x.shape
    f32 = jnp.float32
    u = rms_norm(x, norm_mix_pre)
    proj = u @ w_in
    q, k, v, g, z_f, z_b, pw_a, pw_b, gate_a, gate_b = jnp.split(proj, SPLIT_IDX, axis=-1)

    def heads(t, hd):
        return t.reshape(B, S, N_HEADS, hd).transpose(0, 2, 1, 3).astype(f32)

    qh = heads(q, HEAD_DK) * (HEAD_DK ** -0.5)
    kh = heads(k, HEAD_DK)
    vh = heads(v, HEAD_DV)
    la_f = heads(jax.nn.log_sigmoid((z_f @ w_a_fwd + b_a_fwd).astype(f32)) / GATE_TAU, HEAD_DK)
    la_b = heads(jax.nn.log_sigmoid((z_b @ w_a_bwd + b_a_bwd).astype(f32)) / GATE_TAU, HEAD_DK)
    flip = lambda t: jnp.flip(t, axis=2)
    o_f = gla_scan(qh, kh, vh, la_f)
    o_b = flip(gla_scan(flip(qh), flip(kh), flip(vh), flip(la_b)))
    o = o_f + o_b
    o = o * lax.rsqrt(jnp.mean(o * o, axis=-1, keepdims=True) + EPS)
    o = o * gla_norm.astype(f32).reshape(N_HEADS, 1, HEAD_DV)
    out_a = o.transpose(0, 2, 1, 3).reshape(B, S, D_V).astype(x.dtype) * jax.nn.silu(g)

    glu = pw_a * jax.nn.sigmoid(pw_b)
    dw = lax.conv_general_dilated(
        glu, conv_w.astype(glu.dtype)[:, None, :], window_strides=(1,),
        padding=[(CONV_PAD, CONV_PAD)], dimension_numbers=('NWC', 'WIO', 'NWC'),
        feature_group_count=CONV_DIM) + conv_b
    out_b = jax.nn.silu(layer_norm(dw, conv_ln_g, conv_ln_b)) @ w_pw2

    merged = jax.nn.sigmoid(gate_a) * out_a + jax.nn.sigmoid(gate_b) * out_b
    h = x + rms_norm(merged @ w_out, norm_mix_post)

    f = jnp.square(jax.nn.relu(rms_norm(h, norm_ffn_pre) @ w_ff1)) @ w_ff2
    return h + rms_norm(f, norm_ffn_post)


def setup_inputs(seed: int = 0) -> dict:
    key = jax.random.key(seed)
    ks = jax.random.split(key, 24)
    nrm = lambda k, shape, s: jax.random.normal(k, shape, jnp.float32) * s
    gain = lambda k, n: jnp.ones((n,), jnp.float32) + nrm(k, (n,), 0.02)
    return {
        "x_prompt": nrm(ks[0], (BATCH, SEQ, D_MODEL), 1.0),
        "x_sample": nrm(ks[1], (DEC_BATCH, DEC_SEQ, D_MODEL), 1.0),
        "norm_mix_pre": gain(ks[2], D_MODEL),
        "norm_mix_post": gain(ks[3], D_MODEL),
        "norm_ffn_pre": gain(ks[4], D_MODEL),
        "norm_ffn_post": gain(ks[5], D_MODEL),
        "w_in": nrm(ks[6], (D_MODEL, D_IN), D_MODEL ** -0.5),
        "w_a_fwd": nrm(ks[7], (GATE_RANK, D_K), GATE_RANK ** -0.5),
        "b_a_fwd": nrm(ks[8], (D_K,), 0.1),
        "w_a_bwd": nrm(ks[9], (GATE_RANK, D_K), GATE_RANK ** -0.5),
        "b_a_bwd": nrm(ks[10], (D_K,), 0.1),
        "gla_norm": gain(ks[11], D_V),
        "conv_w": nrm(ks[12], (CONV_WIDTH, CONV_DIM), CONV_WIDTH ** -0.5),
        "conv_b": nrm(ks[13], (CONV_DIM,), 0.02),
        "conv_ln_g": gain(ks[14], CONV_DIM),
        "conv_ln_b": nrm(ks[15], (CONV_DIM,), 0.02),
        "w_pw2": nrm(ks[16], (CONV_DIM, D_MODEL), CONV_DIM ** -0.5),
        "w_out": nrm(ks[17], (D_MODEL, D_MODEL), D_MODEL ** -0.5),
        "w_ff1": nrm(ks[18], (D_MODEL, D_FF), D_MODEL ** -0.5),
        "w_ff2": nrm(ks[19], (D_FF, D_MODEL), D_FF ** -0.5),
    }


def reference(x_prompt, x_sample, norm_mix_pre, norm_mix_post, norm_ffn_pre, norm_ffn_post,
              w_in, w_a_fwd, b_a_fwd, w_a_bwd, b_a_bwd, gla_norm, conv_w, conv_b,
              conv_ln_g, conv_ln_b, w_pw2, w_out, w_ff1, w_ff2):
    y_prompt = x_prompt
    y_sample = x_sample
    for _ in range(DEPTH):
        y_prompt = encoder_layer(y_prompt, norm_mix_pre, norm_mix_post, norm_ffn_pre, norm_ffn_post,
                                 w_in, w_a_fwd, b_a_fwd, w_a_bwd, b_a_bwd, gla_norm, conv_w, conv_b,
                                 conv_ln_g, conv_ln_b, w_pw2, w_out, w_ff1, w_ff2)
        y_sample = encoder_layer(y_sample, norm_mix_pre, norm_mix_post, norm_ffn_pre, norm_ffn_post,
                                 w_in, w_a_fwd, b_a_fwd, w_a_bwd, b_a_bwd, gla_norm, conv_w, conv_b,
                                 conv_ln_g, conv_ln_b, w_pw2, w_out, w_ff1, w_ff2)
    return (y_prompt, y_sample)
```

```python
import functools

import jax
import jax.numpy as jnp
from jax import lax
from jax.experimental import pallas as pl
from jax.experimental.pallas import tpu as pltpu

F32 = jnp.float32
BF16 = jnp.bfloat16

D_MODEL = 2048
N_HEADS = 4
D_K = D_MODEL // 2
D_V = D_MODEL
HEAD_DK = D_K // N_HEADS
HEAD_DV = D_V // N_HEADS
GATE_RANK = 16
GATE_TAU = 16.0
CHUNK = 64
CONV_WIDTH = 31
CONV_PAD = (CONV_WIDTH - 1) // 2
D_FF = 4 * D_MODEL
EPS = 1e-6

P_COLS = 7 * D_MODEL
OFF_G = 2 * D_MODEL
OFF_PWA = 3 * D_MODEL
OFF_PWB = 4 * D_MODEL
OFF_GA = 5 * D_MODEL
OFF_GB = 6 * D_MODEL
Z_COLS = 128

HALO = 16

_MIB = 1024 * 1024


def _cparams(sem, vmem_mib):
    return pltpu.CompilerParams(dimension_semantics=sem, vmem_limit_bytes=vmem_mib * _MIB)


def _sigmoid(x):
    return 1.0 / (1.0 + jnp.exp(-x))


def _log_sigmoid(x):
    return jnp.minimum(x, 0.0) - jnp.log(1.0 + jnp.exp(-jnp.abs(x)))


def _inproj_kernel(x_ref, g_ref, w_ref, wz_ref, p_ref, z_ref, u_ref, *, rows):
    @pl.when(pl.program_id(1) == 0)
    def _():
        def body(r, c):
            sl = pl.ds(pl.multiple_of(r * rows, rows), rows)
            x = x_ref[sl, :]
            ms = jnp.mean(x * x, axis=-1, keepdims=True)
            u = (x * lax.rsqrt(ms + EPS)) * g_ref[...]
            u_ref[sl, :] = u.astype(BF16)
            return c
        lax.fori_loop(0, x_ref.shape[0] // rows, body, 0)
        z_ref[...] = jnp.dot(u_ref[...], wz_ref[...], preferred_element_type=F32)

    p_ref[...] = jnp.dot(u_ref[...], w_ref[...], preferred_element_type=F32).astype(p_ref.dtype)


def _inproj(x2d, g_pre, w_main, w_z, *, tm=1024, tn=1024):
    T = x2d.shape[0]
    grid = (T // tm, P_COLS // tn)
    return pl.pallas_call(
        functools.partial(_inproj_kernel, rows=128),
        grid=grid,
        in_specs=[
            pl.BlockSpec((tm, D_MODEL), lambda i, j: (i, 0)),
            pl.BlockSpec((1, D_MODEL), lambda i, j: (0, 0)),
            pl.BlockSpec((D_MODEL, tn), lambda i, j: (0, j)),
            pl.BlockSpec((D_MODEL, Z_COLS), lambda i, j: (0, 0)),
        ],
        out_specs=[
            pl.BlockSpec((tm, tn), lambda i, j: (i, j)),
            pl.BlockSpec((tm, Z_COLS), lambda i, j: (i, 0)),
        ],
        out_shape=[
            jax.ShapeDtypeStruct((T, P_COLS), BF16),
            jax.ShapeDtypeStruct((T, Z_COLS), F32),
        ],
        scratch_shapes=[pltpu.VMEM((tm, D_MODEL), BF16)],
        compiler_params=_cparams(("parallel", "arbitrary"), 48),
        name="inproj",
    )(x2d, g_pre, w_main, w_z)


def _gla_chunk(q_ref, k_ref, v_ref, la_ref, o_ref, st_ref, start, tri3, mask, ref_idx, last_idx):
    sl = pl.ds(pl.multiple_of(start, CHUNK), CHUNK)
    la = la_ref[sl, :]
    h1 = la.astype(BF16)
    r1 = la - h1.astype(F32)
    h2 = r1.astype(BF16)
    h3 = (r1 - h2.astype(F32)).astype(BF16)
    b = jnp.dot(tri3, jnp.concatenate([h1, h2, h3], axis=0), preferred_element_type=F32)
    b_ref = b[ref_idx:ref_idx + 1, :]
    b_last = b[last_idx:last_idx + 1, :]
    q = q_ref[sl, :].astype(F32) * (HEAD_DK ** -0.5)
    k = k_ref[sl, :].astype(F32)
    v = v_ref[sl, :]
    q_state = (q * jnp.exp(b)).astype(BF16)
    q_intra = (q * jnp.exp(b - b_ref)).astype(BF16)
    k_intra = (k * jnp.exp(b_ref - b)).astype(BF16)
    k_decay = (k * jnp.exp(b_last - b)).astype(BF16)
    st = st_ref[...]
    nt = (((1,), (1,)), ((), ()))
    o_inter = lax.dot_general(q_state, st.astype(BF16), nt, preferred_element_type=F32)
    scores = lax.dot_general(q_intra, k_intra, nt, preferred_element_type=F32)
    scores = jnp.where(mask, scores, 0.0)
    o_intra = jnp.dot(scores.astype(BF16), v, preferred_element_type=F32)
    o_ref[sl, :] = o_inter + o_intra
    tn = (((0,), (0,)), ((), ()))
    upd = lax.dot_general(v, k_decay, tn, preferred_element_type=F32)
    st_ref[...] = jnp.exp(b_last) * st + upd


def _gla_kernel(qf_ref, kf_ref, vf_ref, zf_ref, qb_ref, kb_ref, vb_ref, zb_ref,
                waf_ref, baf_ref, wab_ref, bab_ref, of_ref, ob_ref,
                stf_ref, stb_ref, laf_ref, lab_ref):
    @pl.when(pl.program_id(2) == 0)
    def _():
        stf_ref[...] = jnp.zeros_like(stf_ref)
        stb_ref[...] = jnp.zeros_like(stb_ref)

    def log_decay(z_ref, w_ref, bias_ref):
        pre = jnp.dot(z_ref[...].astype(BF16), w_ref[...], preferred_element_type=F32) + bias_ref[...]
        return _log_sigmoid(pre) / GATE_TAU

    laf_ref[...] = log_decay(zf_ref, waf_ref, baf_ref)
    lab_ref[...] = log_decay(zb_ref, wab_ref, bab_ref)

    row3 = lax.broadcasted_iota(jnp.int32, (CHUNK, 3 * CHUNK), 0)
    col3 = lax.broadcasted_iota(jnp.int32, (CHUNK, 3 * CHUNK), 1) & (CHUNK - 1)
    row = lax.broadcasted_iota(jnp.int32, (CHUNK, CHUNK), 0)
    col = lax.broadcasted_iota(jnp.int32, (CHUNK, CHUNK), 1)
    tril3 = (col3 <= row3).astype(BF16)
    triu3 = (col3 >= row3).astype(BF16)
    lower = col <= row
    upper = col >= row

    n_chunks = qf_ref.shape[0] // CHUNK

    def body(c, carry):
        _gla_chunk(qf_ref, kf_ref, vf_ref, laf_ref, of_ref, stf_ref, c * CHUNK,
                   tril3, lower, CHUNK // 2, CHUNK - 1)
        _gla_chunk(qb_ref, kb_ref, vb_ref, lab_ref, ob_ref, stb_ref, (n_chunks - 1 - c) * CHUNK,
                   triu3, upper, CHUNK - 1 - CHUNK // 2, 0)
        return carry

    lax.fori_loop(0, n_chunks, body, 0)


def _gla(P, Z, wa_f, ba_f, wa_b, ba_b, B, S, *, ts=512):
    T = B * S
    nb = S // ts
    fwd = lambda b, h, i: b * nb + i
    bwd = lambda b, h, i: b * nb + (nb - 1 - i)

    def stream_specs(rowmap):
        return [
            pl.BlockSpec((ts, HEAD_DK), lambda b, h, i: (rowmap(b, h, i), h)),
            pl.BlockSpec((ts, HEAD_DK), lambda b, h, i: (rowmap(b, h, i), N_HEADS + h)),
            pl.BlockSpec((ts, HEAD_DV), lambda b, h, i: (rowmap(b, h, i), N_HEADS + h)),
            pl.BlockSpec((ts, Z_COLS), lambda b, h, i: (rowmap(b, h, i), 0)),
        ]

    w_spec = pl.BlockSpec((Z_COLS, HEAD_DK), lambda b, h, i: (0, h))
    b_spec = pl.BlockSpec((1, HEAD_DK), lambda b, h, i: (0, h))
    return pl.pallas_call(
        _gla_kernel,
        grid=(B, N_HEADS, nb),
        in_specs=stream_specs(fwd) + stream_specs(bwd) + [w_spec, b_spec, w_spec, b_spec],
        out_specs=[
            pl.BlockSpec((ts, HEAD_DV), lambda b, h, i: (fwd(b, h, i), h)),
            pl.BlockSpec((ts, HEAD_DV), lambda b, h, i: (bwd(b, h, i), h)),
        ],
        out_shape=[jax.ShapeDtypeStruct((T, D_V), F32)] * 2,
        scratch_shapes=[
            pltpu.VMEM((HEAD_DV, HEAD_DK), F32),
            pltpu.VMEM((HEAD_DV, HEAD_DK), F32),
            pltpu.VMEM((ts, HEAD_DK), F32),
            pltpu.VMEM((ts, HEAD_DK), F32),
        ],
        compiler_params=_cparams(("parallel", "parallel", "arbitrary"), 32),
        name="gla",
    )(P, P, P, Z, P, P, P, Z, wa_f, ba_f, wa_b, ba_b)


def _conv_kernel(pa_ref, pb_ref, pa_prev, pb_prev, pa_next, pb_next,
                 cw_ref, cb_ref, lg_ref, lb_ref, w2_ref, out_ref,
                 glu_ref, dw_ref, act_ref, *, rb, lb):
    i = pl.program_id(1)
    last = pl.num_programs(1) - 1
    tt = pa_ref.shape[0]

    def glu(a, b):
        return a.astype(F32) * _sigmoid(b.astype(F32))

    n_win = tt // rb
    glu_ref[0, 0:HALO, :] = jnp.where(i > 0, glu(pa_prev[...], pb_prev[...]), 0.0)
    glu_ref[0, HALO:rb, :] = glu(pa_ref[0:rb - HALO, :], pb_ref[0:rb - HALO, :])
    glu_ref[n_win - 1, rb:2 * rb - HALO, :] = glu(pa_ref[tt - (rb - HALO):tt, :], pb_ref[tt - (rb - HALO):tt, :])
    glu_ref[n_win - 1, 2 * rb - HALO:2 * rb, :] = jnp.where(i < last, glu(pa_next[...], pb_next[...]), 0.0)

    def fill_body(g, c):
        src = pl.ds(pl.multiple_of(g * rb - HALO, HALO), rb)
        val = glu(pa_ref[src, :], pb_ref[src, :])
        glu_ref[g, 0:rb, :] = val
        glu_ref[g - 1, rb:2 * rb, :] = val
        return c

    lax.fori_loop(1, n_win, fill_body, 0)

    n_lb = D_MODEL // lb

    def conv_body(it, c):
        m = it // n_lb
        lanes = pl.ds(pl.multiple_of((it % n_lb) * lb, lb), lb)
        acc = jnp.zeros((rb, lb), F32)
        for j in range(CONV_WIDTH):
            acc = acc + cw_ref[j:j + 1, lanes] * glu_ref[m, pl.ds(j + HALO - CONV_PAD, rb), lanes]
        dw_ref[pl.ds(pl.multiple_of(m * rb, rb), rb), lanes] = acc + cb_ref[:, lanes]
        return c

    lax.fori_loop(0, n_win * n_lb, conv_body, 0)

    ln_rows = 32

    def ln_body(r, c):
        sl = pl.ds(pl.multiple_of(r * ln_rows, ln_rows), ln_rows)
        x = dw_ref[sl, :]
        mu = jnp.mean(x, axis=-1, keepdims=True)
        xc = x - mu
        var = jnp.mean(xc * xc, axis=-1, keepdims=True)
        y = xc * lax.rsqrt(var + EPS) * lg_ref[...] + lb_ref[...]
        act_ref[sl, :] = (y * _sigmoid(y)).astype(BF16)
        return c

    lax.fori_loop(0, tt // ln_rows, ln_body, 0)
    out_ref[...] = jnp.dot(act_ref[...], w2_ref[...], preferred_element_type=F32).astype(out_ref.dtype)


def _conv_module(P, conv_w, conv_b, ln_g, ln_b, w_pw2, B, S, *, tt=512):
    T = B * S
    nt = S // tt
    hb = tt // HALO
    n_halo_blocks = T // HALO
    ca, cb = OFF_PWA // D_MODEL, OFF_PWB // D_MODEL
    row = lambda b, i: b * nt + i
    prev = lambda b, i: jnp.maximum(row(b, i) * hb - 1, 0)
    nxt = lambda b, i: jnp.minimum((row(b, i) + 1) * hb, n_halo_blocks - 1)
    vec = lambda n: pl.BlockSpec((n, D_MODEL), lambda b, i: (0, 0))
    rb = 2 * HALO
    return pl.pallas_call(
        functools.partial(_conv_kernel, rb=rb, lb=256),
        grid=(B, nt),
        in_specs=[
            pl.BlockSpec((tt, D_MODEL), lambda b, i: (row(b, i), ca)),
            pl.BlockSpec((tt, D_MODEL), lambda b, i: (row(b, i), cb)),
            pl.BlockSpec((HALO, D_MODEL), lambda b, i: (prev(b, i), ca)),
            pl.BlockSpec((HALO, D_MODEL), lambda b, i: (prev(b, i), cb)),
            pl.BlockSpec((HALO, D_MODEL), lambda b, i: (nxt(b, i), ca)),
            pl.BlockSpec((HALO, D_MODEL), lambda b, i: (nxt(b, i), cb)),
            vec(CONV_WIDTH), vec(1), vec(1), vec(1),
            pl.BlockSpec((D_MODEL, D_MODEL), lambda b, i: (0, 0)),
        ],
        out_specs=pl.BlockSpec((tt, D_MODEL), lambda b, i: (row(b, i), 0)),
        out_shape=jax.ShapeDtypeStruct((T, D_MODEL), BF16),
        scratch_shapes=[
            pltpu.VMEM((tt // rb, 2 * rb, D_MODEL), F32),
            pltpu.VMEM((tt, D_MODEL), F32),
            pltpu.VMEM((tt, D_MODEL), BF16),
        ],
        compiler_params=_cparams(("parallel", "parallel"), 48),
        name="conv_module",
    )(P, P, P, P, P, P, conv_w, conv_b, ln_g, ln_b, w_pw2)


def _merge_kernel(of_ref, ob_ref, g_ref, ga_ref, gb_ref, cb_ref, x_ref, gn_ref, gpost_ref, wo_ref,
                  h_ref, m_ref, *, rows):
    tt = x_ref.shape[0]

    def merge_body(r, c):
        sl = pl.ds(pl.multiple_of(r * rows, rows), rows)
        o = of_ref[sl, :] + ob_ref[sl, :]
        heads = []
        for hd in range(N_HEADS):
            oh = o[:, hd * HEAD_DV:(hd + 1) * HEAD_DV]
            ms = jnp.mean(oh * oh, axis=-1, keepdims=True)
            heads.append(oh * lax.rsqrt(ms + EPS))
        on = jnp.concatenate(heads, axis=-1) * gn_ref[...]
        g = g_ref[sl, :].astype(F32)
        out_a = on * (g * _sigmoid(g))
        merged = (_sigmoid(ga_ref[sl, :].astype(F32)) * out_a
                  + _sigmoid(gb_ref[sl, :].astype(F32)) * cb_ref[sl, :].astype(F32))
        m_ref[sl, :] = merged.astype(BF16)
        return c

    lax.fori_loop(0, tt // rows, merge_body, 0)
    h_ref[...] = jnp.dot(m_ref[...], wo_ref[...], preferred_element_type=F32)

    def norm_body(r, c):
        sl = pl.ds(pl.multiple_of(r * rows, rows), rows)
        y = h_ref[sl, :]
        ms = jnp.mean(y * y, axis=-1, keepdims=True)
        h_ref[sl, :] = x_ref[sl, :] + (y * lax.rsqrt(ms + EPS)) * gpost_ref[...]
        return c

    lax.fori_loop(0, tt // rows, norm_body, 0)


def _merge(o_f, o_b, P, out_b, x2d, gla_norm, g_post, w_out, *, tt=256):
    T = x2d.shape[0]
    tile = lambda c: pl.BlockSpec((tt, D_MODEL), lambda i: (i, c))
    vec = pl.BlockSpec((1, D_MODEL), lambda i: (0, 0))
    return pl.pallas_call(
        functools.partial(_merge_kernel, rows=32),
        grid=(T // tt,),
        in_specs=[
            tile(0), tile(0),
            tile(OFF_G // D_MODEL), tile(OFF_GA // D_MODEL), tile(OFF_GB // D_MODEL),
            tile(0), tile(0), vec, vec,
            pl.BlockSpec((D_MODEL, D_MODEL), lambda i: (0, 0)),
        ],
        out_specs=tile(0),
        out_shape=jax.ShapeDtypeStruct((T, D_MODEL), F32),
        scratch_shapes=[pltpu.VMEM((tt, D_MODEL), BF16)],
        compiler_params=_cparams(("parallel",), 48),
        name="merge_outproj",
    )(o_f, o_b, P, P, P, out_b, x2d, gla_norm, g_post, w_out)


def _ffn_kernel(h_ref, gpre_ref, gpost_ref, w1_ref, w2_ref, y_ref, u_ref, acc_ref, *, rows):
    f = pl.program_id(1)
    tm = h_ref.shape[0]

    @pl.when(f == 0)
    def _():
        def body(r, c):
            sl = pl.ds(pl.multiple_of(r * rows, rows), rows)
            x = h_ref[sl, :]
            ms = jnp.mean(x * x, axis=-1, keepdims=True)
            u_ref[sl, :] = ((x * lax.rsqrt(ms + EPS)) * gpre_ref[...]).astype(BF16)
            return c
        lax.fori_loop(0, tm // rows, body, 0)
        acc_ref[...] = jnp.zeros_like(acc_ref)

    a = jnp.dot(u_ref[...], w1_ref[...], preferred_element_type=F32)
    a = jnp.maximum(a, 0.0)
    a = (a * a).astype(BF16)
    acc_ref[...] += jnp.dot(a, w2_ref[...], preferred_element_type=F32)

    @pl.when(f == pl.num_programs(1) - 1)
    def _():
        def body(r, c):
            sl = pl.ds(pl.multiple_of(r * rows, rows), rows)
            y = acc_ref[sl, :]
            ms = jnp.mean(y * y, axis=-1, keepdims=True)
            y_ref[sl, :] = h_ref[sl, :] + (y * lax.rsqrt(ms + EPS)) * gpost_ref[...]
            return c
        lax.fori_loop(0, tm // rows, body, 0)


def _ffn(h2d, g_pre, g_post, w1, w2, *, tm=512, tf=1024):
    T = h2d.shape[0]
    vec = pl.BlockSpec((1, D_MODEL), lambda i, f: (0, 0))
    return pl.pallas_call(
        functools.partial(_ffn_kernel, rows=128),
        grid=(T // tm, D_FF // tf),
        in_specs=[
            pl.BlockSpec((tm, D_MODEL), lambda i, f: (i, 0)),
            vec, vec,
            pl.BlockSpec((D_MODEL, tf), lambda i, f: (0, f)),
            pl.BlockSpec((tf, D_MODEL), lambda i, f: (f, 0)),
        ],
        out_specs=pl.BlockSpec((tm, D_MODEL), lambda i, f: (i, 0)),
        out_shape=jax.ShapeDtypeStruct((T, D_MODEL), F32),
        scratch_shapes=[pltpu.VMEM((tm, D_MODEL), BF16), pltpu.VMEM((tm, D_MODEL), F32)],
        compiler_params=_cparams(("parallel", "arbitrary"), 48),
        name="ffn",
    )(h2d, g_pre, g_post, w1, w2)


def _encoder_layer(x, wts):
    B, S, _ = x.shape
    x2d = x.reshape(B * S, D_MODEL)
    P, Z = _inproj(x2d, wts["g_mix_pre"], wts["w_main"], wts["w_z"])
    o_f, o_b = _gla(P, Z, wts["wa_f"], wts["ba_f"], wts["wa_b"], wts["ba_b"], B, S)
    out_b = _conv_module(P, wts["conv_w"], wts["conv_b"], wts["ln_g"], wts["ln_b"], wts["w_pw2"], B, S)
    h = _merge(o_f, o_b, P, out_b, x2d, wts["gla_norm"], wts["g_mix_post"], wts["w_out"])
    y = _ffn(h, wts["g_ffn_pre"], wts["g_ffn_post"], wts["w_ff1"], wts["w_ff2"])
    return y.reshape(B, S, D_MODEL)


def kernel(x_prompt, x_sample, norm_mix_pre, norm_mix_post, norm_ffn_pre, norm_ffn_post, w_in, w_a_fwd, b_a_fwd, w_a_bwd, b_a_bwd, gla_norm, conv_w, conv_b, conv_ln_g, conv_ln_b, w_pw2, w_out, w_ff1, w_ff2):
    z0 = 2 * D_K + 2 * D_V
    z1 = z0 + 2 * GATE_RANK
    row = lambda v: v.reshape(1, -1).astype(F32)
    w_z = jnp.zeros((D_MODEL, Z_COLS), F32).at[:, :2 * GATE_RANK].set(w_in[:, z0:z1])
    wa_f = jnp.zeros((Z_COLS, D_K), F32).at[:GATE_RANK].set(w_a_fwd)
    wa_b = jnp.zeros((Z_COLS, D_K), F32).at[GATE_RANK:2 * GATE_RANK].set(w_a_bwd)
    wts = {
        "g_mix_pre": row(norm_mix_pre), "g_mix_post": row(norm_mix_post),
        "g_ffn_pre": row(norm_ffn_pre), "g_ffn_post": row(norm_ffn_post),
        "w_main": jnp.concatenate([w_in[:, :z0], w_in[:, z1:]], axis=1).astype(BF16),
        "w_z": w_z.astype(BF16),
        "wa_f": wa_f.astype(BF16), "ba_f": row(b_a_fwd),
        "wa_b": wa_b.astype(BF16), "ba_b": row(b_a_bwd),
        "gla_norm": row(gla_norm),
        "conv_w": conv_w.astype(F32), "conv_b": row(conv_b),
        "ln_g": row(conv_ln_g), "ln_b": row(conv_ln_b),
        "w_pw2": w_pw2.astype(BF16), "w_out": w_out.astype(BF16),
        "w_ff1": w_ff1.astype(BF16), "w_ff2": w_ff2.astype(BF16),
    }
    return (_encoder_layer(x_prompt, wts), _encoder_layer(x_sample, wts))
```

```python
import functools

import jax
import jax.numpy as jnp
from jax import lax
from jax.experimental import pallas as pl
from jax.experimental.pallas import tpu as pltpu

F32 = jnp.float32
BF16 = jnp.bfloat16

D_MODEL = 2048
N_HEADS = 4
D_K = D_MODEL // 2
D_V = D_MODEL
HEAD_DK = D_K // N_HEADS
HEAD_DV = D_V // N_HEADS
GATE_RANK = 16
GATE_TAU = 16.0
CHUNK = 64
CONV_WIDTH = 31
CONV_PAD = (CONV_WIDTH - 1) // 2
D_FF = 4 * D_MODEL
EPS = 1e-6

P_COLS = 7 * D_MODEL
OFF_G = 2 * D_MODEL
OFF_PWA = 3 * D_MODEL
OFF_PWB = 4 * D_MODEL
OFF_GA = 5 * D_MODEL
OFF_GB = 6 * D_MODEL
Z_COLS = 128

HALO = 16

_MIB = 1024 * 1024


def _cparams(sem, vmem_mib):
    return pltpu.CompilerParams(dimension_semantics=sem, vmem_limit_bytes=vmem_mib * _MIB)


def _sigmoid(x):
    return 1.0 / (1.0 + jnp.exp(-x))


def _log_sigmoid(x):
    return jnp.minimum(x, 0.0) - jnp.log(1.0 + jnp.exp(-jnp.abs(x)))


def _inproj_kernel(x_ref, g_ref, w_ref, wz_ref, p_ref, z_ref, u_ref, *, rows):
    @pl.when(pl.program_id(1) == 0)
    def _():
        def body(r, c):
            sl = pl.ds(pl.multiple_of(r * rows, rows), rows)
            x = x_ref[sl, :]
            ms = jnp.mean(x * x, axis=-1, keepdims=True)
            u = (x * lax.rsqrt(ms + EPS)) * g_ref[...]
            u_ref[sl, :] = u.astype(BF16)
            return c
        lax.fori_loop(0, x_ref.shape[0] // rows, body, 0)
        z_ref[...] = jnp.dot(u_ref[...], wz_ref[...], preferred_element_type=F32)

    p_ref[...] = jnp.dot(u_ref[...], w_ref[...], preferred_element_type=F32).astype(p_ref.dtype)


def _inproj(x2d, g_pre, w_main, w_z, *, tm=1024, tn=1024):
    T = x2d.shape[0]
    grid = (T // tm, P_COLS // tn)
    return pl.pallas_call(
        functools.partial(_inproj_kernel, rows=128),
        grid=grid,
        in_specs=[
            pl.BlockSpec((tm, D_MODEL), lambda i, j: (i, 0)),
            pl.BlockSpec((1, D_MODEL), lambda i, j: (0, 0)),
            pl.BlockSpec((D_MODEL, tn), lambda i, j: (0, j)),
            pl.BlockSpec((D_MODEL, Z_COLS), lambda i, j: (0, 0)),
        ],
        out_specs=[
            pl.BlockSpec((tm, tn), lambda i, j: (i, j)),
            pl.BlockSpec((tm, Z_COLS), lambda i, j: (i, 0)),
        ],
        out_shape=[
            jax.ShapeDtypeStruct((T, P_COLS), BF16),
            jax.ShapeDtypeStruct((T, Z_COLS), F32),
        ],
        scratch_shapes=[pltpu.VMEM((tm, D_MODEL), BF16)],
        compiler_params=_cparams(("parallel", "arbitrary"), 48),
        name="inproj",
    )(x2d, g_pre, w_main, w_z)


def _gla_chunk(q_ref, k_ref, v_ref, la_ref, o_ref, st_ref, start, tri3, mask, ref_idx, last_idx):
    sl = pl.ds(start, CHUNK)
    la = la_ref[sl, :]
    h1 = la.astype(BF16)
    r1 = la - h1.astype(F32)
    h2 = r1.astype(BF16)
    h3 = (r1 - h2.astype(F32)).astype(BF16)
    b = jnp.dot(tri3, jnp.concatenate([h1, h2, h3], axis=0), preferred_element_type=F32)
    b_ref = b[ref_idx:ref_idx + 1, :]
    b_last = b[last_idx:last_idx + 1, :]
    q = q_ref[sl, :].astype(F32) * (HEAD_DK ** -0.5)
    k = k_ref[sl, :].astype(F32)
    v = v_ref[sl, :]
    q_state = (q * jnp.exp(b)).astype(BF16)
    q_intra = (q * jnp.exp(b - b_ref)).astype(BF16)
    k_intra = (k * jnp.exp(b_ref - b)).astype(BF16)
    k_decay = (k * jnp.exp(b_last - b)).astype(BF16)
    st = st_ref[...]
    nt = (((1,), (1,)), ((), ()))
    o_inter = lax.dot_general(q_state, st.astype(BF16), nt, preferred_element_type=F32)
    scores = lax.dot_general(q_intra, k_intra, nt, preferred_element_type=F32)
    scores = jnp.where(mask, scores, 0.0)
    o_intra = jnp.dot(scores.astype(BF16), v, preferred_element_type=F32)
    o_ref[sl, :] = o_inter + o_intra
    tn = (((0,), (0,)), ((), ()))
    upd = lax.dot_general(v, k_decay, tn, preferred_element_type=F32)
    st_ref[...] = jnp.exp(b_last) * st + upd


def _gla_kernel(qf_ref, kf_ref, vf_ref, zf_ref, qb_ref, kb_ref, vb_ref, zb_ref,
                waf_ref, baf_ref, wab_ref, bab_ref, of_ref, ob_ref,
                stf_ref, stb_ref, laf_ref, lab_ref):
    @pl.when(pl.program_id(2) == 0)
    def _():
        stf_ref[...] = jnp.zeros_like(stf_ref)
        stb_ref[...] = jnp.zeros_like(stb_ref)

    def log_decay(z_ref, w_ref, bias_ref):
        pre = jnp.dot(z_ref[...].astype(BF16), w_ref[...], preferred_element_type=F32) + bias_ref[...]
        return _log_sigmoid(pre) / GATE_TAU

    laf_ref[...] = log_decay(zf_ref, waf_ref, baf_ref)
    lab_ref[...] = log_decay(zb_ref, wab_ref, bab_ref)

    row3 = lax.broadcasted_iota(jnp.int32, (CHUNK, 3 * CHUNK), 0)
    col3 = lax.broadcasted_iota(jnp.int32, (CHUNK, 3 * CHUNK), 1) & (CHUNK - 1)
    row = lax.broadcasted_iota(jnp.int32, (CHUNK, CHUNK), 0)
    col = lax.broadcasted_iota(jnp.int32, (CHUNK, CHUNK), 1)
    tril3 = (col3 <= row3).astype(BF16)
    triu3 = (col3 >= row3).astype(BF16)
    lower = col <= row
    upper = col >= row

    n_chunks = qf_ref.shape[0] // CHUNK

    for c in range(n_chunks):
        _gla_chunk(qf_ref, kf_ref, vf_ref, laf_ref, of_ref, stf_ref, c * CHUNK,
                   tril3, lower, CHUNK // 2, CHUNK - 1)
        _gla_chunk(qb_ref, kb_ref, vb_ref, lab_ref, ob_ref, stb_ref, (n_chunks - 1 - c) * CHUNK,
                   triu3, upper, CHUNK - 1 - CHUNK // 2, 0)


def _gla(P, Z, wa_f, ba_f, wa_b, ba_b, B, S, *, ts=512):
    T = B * S
    nb = S // ts
    fwd = lambda b, h, i: b * nb + i
    bwd = lambda b, h, i: b * nb + (nb - 1 - i)

    def stream_specs(rowmap):
        return [
            pl.BlockSpec((ts, HEAD_DK), lambda b, h, i: (rowmap(b, h, i), h)),
            pl.BlockSpec((ts, HEAD_DK), lambda b, h, i: (rowmap(b, h, i), N_HEADS + h)),
            pl.BlockSpec((ts, HEAD_DV), lambda b, h, i: (rowmap(b, h, i), N_HEADS + h)),
            pl.BlockSpec((ts, Z_COLS), lambda b, h, i: (rowmap(b, h, i), 0)),
        ]

    w_spec = pl.BlockSpec((Z_COLS, HEAD_DK), lambda b, h, i: (0, h))
    b_spec = pl.BlockSpec((1, HEAD_DK), lambda b, h, i: (0, h))
    return pl.pallas_call(
        _gla_kernel,
        grid=(B, N_HEADS, nb),
        in_specs=stream_specs(fwd) + stream_specs(bwd) + [w_spec, b_spec, w_spec, b_spec],
        out_specs=[
            pl.BlockSpec((ts, HEAD_DV), lambda b, h, i: (fwd(b, h, i), h)),
            pl.BlockSpec((ts, HEAD_DV), lambda b, h, i: (bwd(b, h, i), h)),
        ],
        out_shape=[jax.ShapeDtypeStruct((T, D_V), F32)] * 2,
        scratch_shapes=[
            pltpu.VMEM((HEAD_DV, HEAD_DK), F32),
            pltpu.VMEM((HEAD_DV, HEAD_DK), F32),
            pltpu.VMEM((ts, HEAD_DK), F32),
            pltpu.VMEM((ts, HEAD_DK), F32),
        ],
        compiler_params=_cparams(("parallel", "parallel", "arbitrary"), 32),
        name="gla",
    )(P, P, P, Z, P, P, P, Z, wa_f, ba_f, wa_b, ba_b)


def _conv_kernel(pa_ref, pb_ref, pa_prev, pb_prev, pa_next, pb_next,
                 cw_ref, cb_ref, lg_ref, lb_ref, w2_ref, out_ref,
                 glu_ref, dw_ref, act_ref, *, rb, lb):
    i = pl.program_id(1)
    last = pl.num_programs(1) - 1
    tt = pa_ref.shape[0]

    def glu(a, b):
        return a.astype(F32) * _sigmoid(b.astype(F32))

    n_win = tt // rb
    head = rb + HALO
    glu_ref[0, 0:HALO, :] = jnp.where(i > 0, glu(pa_prev[...], pb_prev[...]), 0.0)
    glu_ref[0, HALO:HALO + head, :] = glu(pa_ref[0:head, :], pb_ref[0:head, :])
    glu_ref[n_win - 1, 0:head, :] = glu(pa_ref[tt - head:tt, :], pb_ref[tt - head:tt, :])
    glu_ref[n_win - 1, head:head + HALO, :] = jnp.where(i < last, glu(pa_next[...], pb_next[...]), 0.0)

    def fill_body(m, c):
        src = pl.ds(pl.multiple_of(m * rb - HALO, HALO), rb + 2 * HALO)
        glu_ref[m] = glu(pa_ref[src, :], pb_ref[src, :])
        return c

    lax.fori_loop(1, n_win - 1, fill_body, 0)

    n_lb = D_MODEL // lb
    sub = 8
    shift = HALO - CONV_PAD

    def conv_body(it, c):
        m = it // n_lb
        lanes = pl.ds(pl.multiple_of((it % n_lb) * lb, lb), lb)
        acc = None
        for s in range(sub):
            part = None
            for q in range((CONV_WIDTH + shift) // sub + 1):
                j = sub * q + s - shift
                if 0 <= j < CONV_WIDTH:
                    term = cw_ref[j:j + 1, lanes] * glu_ref[m, pl.ds(sub * q, rb + sub), lanes]
                    part = term if part is None else part + term
            shifted = part[s:s + rb]
            acc = shifted if acc is None else acc + shifted
        dw_ref[pl.ds(pl.multiple_of(m * rb, rb), rb), lanes] = acc + cb_ref[:, lanes]
        return c

    lax.fori_loop(0, n_win * n_lb, conv_body, 0)

    ln_rows = 32

    def ln_body(r, c):
        sl = pl.ds(pl.multiple_of(r * ln_rows, ln_rows), ln_rows)
        x = dw_ref[sl, :]
        mu = jnp.mean(x, axis=-1, keepdims=True)
        xc = x - mu
        var = jnp.mean(xc * xc, axis=-1, keepdims=True)
        y = xc * lax.rsqrt(var + EPS) * lg_ref[...] + lb_ref[...]
        act_ref[sl, :] = (y * _sigmoid(y)).astype(BF16)
        return c

    lax.fori_loop(0, tt // ln_rows, ln_body, 0)
    out_ref[...] = jnp.dot(act_ref[...], w2_ref[...], preferred_element_type=F32).astype(out_ref.dtype)


def _conv_module(P, conv_w, conv_b, ln_g, ln_b, w_pw2, B, S, *, tt=512):
    T = B * S
    nt = S // tt
    hb = tt // HALO
    n_halo_blocks = T // HALO
    ca, cb = OFF_PWA // D_MODEL, OFF_PWB // D_MODEL
    row = lambda b, i: b * nt + i
    prev = lambda b, i: jnp.maximum(row(b, i) * hb - 1, 0)
    nxt = lambda b, i: jnp.minimum((row(b, i) + 1) * hb, n_halo_blocks - 1)
    vec = lambda n: pl.BlockSpec((n, D_MODEL), lambda b, i: (0, 0))
    rb = 64
    return pl.pallas_call(
        functools.partial(_conv_kernel, rb=rb, lb=128),
        grid=(B, nt),
        in_specs=[
            pl.BlockSpec((tt, D_MODEL), lambda b, i: (row(b, i), ca)),
            pl.BlockSpec((tt, D_MODEL), lambda b, i: (row(b, i), cb)),
            pl.BlockSpec((HALO, D_MODEL), lambda b, i: (prev(b, i), ca)),
            pl.BlockSpec((HALO, D_MODEL), lambda b, i: (prev(b, i), cb)),
            pl.BlockSpec((HALO, D_MODEL), lambda b, i: (nxt(b, i), ca)),
            pl.BlockSpec((HALO, D_MODEL), lambda b, i: (nxt(b, i), cb)),
            vec(CONV_WIDTH), vec(1), vec(1), vec(1),
            pl.BlockSpec((D_MODEL, D_MODEL), lambda b, i: (0, 0)),
        ],
        out_specs=pl.BlockSpec((tt, D_MODEL), lambda b, i: (row(b, i), 0)),
        out_shape=jax.ShapeDtypeStruct((T, D_MODEL), BF16),
        scratch_shapes=[
            pltpu.VMEM((tt // rb, rb + 2 * HALO, D_MODEL), F32),
            pltpu.VMEM((tt, D_MODEL), F32),
            pltpu.VMEM((tt, D_MODEL), BF16),
        ],
        compiler_params=_cparams(("parallel", "parallel"), 48),
        name="conv_module",
    )(P, P, P, P, P, P, conv_w, conv_b, ln_g, ln_b, w_pw2)


def _merge_kernel(of_ref, ob_ref, g_ref, ga_ref, gb_ref, cb_ref, x_ref, gn_ref, gpost_ref, wo_ref,
                  h_ref, m_ref, *, rows):
    tt = x_ref.shape[0]

    def merge_body(r, c):
        sl = pl.ds(pl.multiple_of(r * rows, rows), rows)
        o = of_ref[sl, :] + ob_ref[sl, :]
        heads = []
        for hd in range(N_HEADS):
            oh = o[:, hd * HEAD_DV:(hd + 1) * HEAD_DV]
            ms = jnp.mean(oh * oh, axis=-1, keepdims=True)
            heads.append(oh * lax.rsqrt(ms + EPS))
        on = jnp.concatenate(heads, axis=-1) * gn_ref[...]
        g = g_ref[sl, :].astype(F32)
        out_a = on * (g * _sigmoid(g))
        merged = (_sigmoid(ga_ref[sl, :].astype(F32)) * out_a
                  + _sigmoid(gb_ref[sl, :].astype(F32)) * cb_ref[sl, :].astype(F32))
        m_ref[sl, :] = merged.astype(BF16)
        return c

    lax.fori_loop(0, tt // rows, merge_body, 0)
    h_ref[...] = jnp.dot(m_ref[...], wo_ref[...], preferred_element_type=F32)

    def norm_body(r, c):
        sl = pl.ds(pl.multiple_of(r * rows, rows), rows)
        y = h_ref[sl, :]
        ms = jnp.mean(y * y, axis=-1, keepdims=True)
        h_ref[sl, :] = x_ref[sl, :] + (y * lax.rsqrt(ms + EPS)) * gpost_ref[...]
        return c

    lax.fori_loop(0, tt // rows, norm_body, 0)


def _merge(o_f, o_b, P, out_b, x2d, gla_norm, g_post, w_out, *, tt=256):
    T = x2d.shape[0]
    tile = lambda c: pl.BlockSpec((tt, D_MODEL), lambda i: (i, c))
    vec = pl.BlockSpec((1, D_MODEL), lambda i: (0, 0))
    return pl.pallas_call(
        functools.partial(_merge_kernel, rows=32),
        grid=(T // tt,),
        in_specs=[
            tile(0), tile(0),
            tile(OFF_G // D_MODEL), tile(OFF_GA // D_MODEL), tile(OFF_GB // D_MODEL),
            tile(0), tile(0), vec, vec,
            pl.BlockSpec((D_MODEL, D_MODEL), lambda i: (0, 0)),
        ],
        out_specs=tile(0),
        out_shape=jax.ShapeDtypeStruct((T, D_MODEL), F32),
        scratch_shapes=[pltpu.VMEM((tt, D_MODEL), BF16)],
        compiler_params=_cparams(("parallel",), 48),
        name="merge_outproj",
    )(o_f, o_b, P, P, P, out_b, x2d, gla_norm, g_post, w_out)


def _ffn_kernel(h_ref, gpre_ref, gpost_ref, w1_ref, w2_ref, y_ref, u_ref, acc_ref, *, rows):
    f = pl.program_id(1)
    tm = h_ref.shape[0]

    @pl.when(f == 0)
    def _():
        def body(r, c):
            sl = pl.ds(pl.multiple_of(r * rows, rows), rows)
            x = h_ref[sl, :]
            ms = jnp.mean(x * x, axis=-1, keepdims=True)
            u_ref[sl, :] = ((x * lax.rsqrt(ms + EPS)) * gpre_ref[...]).astype(BF16)
            return c
        lax.fori_loop(0, tm // rows, body, 0)
        acc_ref[...] = jnp.zeros_like(acc_ref)

    a = jnp.dot(u_ref[...], w1_ref[...], preferred_element_type=F32)
    a = jnp.maximum(a, 0.0)
    a = (a * a).astype(BF16)
    acc_ref[...] += jnp.dot(a, w2_ref[...], preferred_element_type=F32)

    @pl.when(f == pl.num_programs(1) - 1)
    def _():
        def body(r, c):
            sl = pl.ds(pl.multiple_of(r * rows, rows), rows)
            y = acc_ref[sl, :]
            ms = jnp.mean(y * y, axis=-1, keepdims=True)
            y_ref[sl, :] = h_ref[sl, :] + (y * lax.rsqrt(ms + EPS)) * gpost_ref[...]
            return c
        lax.fori_loop(0, tm // rows, body, 0)


def _ffn(h2d, g_pre, g_post, w1, w2, *, tm=512, tf=1024):
    T = h2d.shape[0]
    vec = pl.BlockSpec((1, D_MODEL), lambda i, f: (0, 0))
    return pl.pallas_call(
        functools.partial(_ffn_kernel, rows=128),
        grid=(T // tm, D_FF // tf),
        in_specs=[
            pl.BlockSpec((tm, D_MODEL), lambda i, f: (i, 0)),
            vec, vec,
            pl.BlockSpec((D_MODEL, tf), lambda i, f: (0, f)),
            pl.BlockSpec((tf, D_MODEL), lambda i, f: (f, 0)),
        ],
        out_specs=pl.BlockSpec((tm, D_MODEL), lambda i, f: (i, 0)),
        out_shape=jax.ShapeDtypeStruct((T, D_MODEL), F32),
        scratch_shapes=[pltpu.VMEM((tm, D_MODEL), BF16), pltpu.VMEM((tm, D_MODEL), F32)],
        compiler_params=_cparams(("parallel", "arbitrary"), 48),
        name="ffn",
    )(h2d, g_pre, g_post, w1, w2)


def _encoder_layer(x, wts):
    B, S, _ = x.shape
    x2d = x.reshape(B * S, D_MODEL)
    P, Z = _inproj(x2d, wts["g_mix_pre"], wts["w_main"], wts["w_z"])
    o_f, o_b = _gla(P, Z, wts["wa_f"], wts["ba_f"], wts["wa_b"], wts["ba_b"], B, S)
    out_b = _conv_module(P, wts["conv_w"], wts["conv_b"], wts["ln_g"], wts["ln_b"], wts["w_pw2"], B, S)
    h = _merge(o_f, o_b, P, out_b, x2d, wts["gla_norm"], wts["g_mix_post"], wts["w_out"])
    y = _ffn(h, wts["g_ffn_pre"], wts["g_ffn_post"], wts["w_ff1"], wts["w_ff2"])
    return y.reshape(B, S, D_MODEL)


def kernel(x_prompt, x_sample, norm_mix_pre, norm_mix_post, norm_ffn_pre, norm_ffn_post, w_in, w_a_fwd, b_a_fwd, w_a_bwd, b_a_bwd, gla_norm, conv_w, conv_b, conv_ln_g, conv_ln_b, w_pw2, w_out, w_ff1, w_ff2):
    z0 = 2 * D_K + 2 * D_V
    z1 = z0 + 2 * GATE_RANK
    row = lambda v: v.reshape(1, -1).astype(F32)
    w_z = jnp.zeros((D_MODEL, Z_COLS), F32).at[:, :2 * GATE_RANK].set(w_in[:, z0:z1])
    wa_f = jnp.zeros((Z_COLS, D_K), F32).at[:GATE_RANK].set(w_a_fwd)
    wa_b = jnp.zeros((Z_COLS, D_K), F32).at[GATE_RANK:2 * GATE_RANK].set(w_a_bwd)
    wts = {
        "g_mix_pre": row(norm_mix_pre), "g_mix_post": row(norm_mix_post),
        "g_ffn_pre": row(norm_ffn_pre), "g_ffn_post": row(norm_ffn_post),
        "w_main": jnp.concatenate([w_in[:, :z0], w_in[:, z1:]], axis=1).astype(BF16),
        "w_z": w_z.astype(BF16),
        "wa_f": wa_f.astype(BF16), "ba_f": row(b_a_fwd),
        "wa_b": wa_b.astype(BF16), "ba_b": row(b_a_bwd),
        "gla_norm": row(gla_norm),
        "conv_w": conv_w.astype(F32), "conv_b": row(conv_b),
        "ln_g": row(conv_ln_g), "ln_b": row(conv_ln_b),
        "w_pw2": w_pw2.astype(BF16), "w_out": w_out.astype(BF16),
        "w_ff1": w_ff1.astype(BF16), "w_ff2": w_ff2.astype(BF16),
    }
    return (_encoder_layer(x_prompt, wts), _encoder_layer(x_sample, wts))
```

```python
import functools

import jax
import jax.numpy as jnp
from jax import lax
from jax.experimental import pallas as pl
from jax.experimental.pallas import tpu as pltpu

F32 = jnp.float32
BF16 = jnp.bfloat16

D_MODEL = 2048
N_HEADS = 4
D_K = D_MODEL // 2
D_V = D_MODEL
HEAD_DK = D_K // N_HEADS
HEAD_DV = D_V // N_HEADS
GATE_RANK = 16
GATE_TAU = 16.0
CHUNK = 64
CONV_WIDTH = 31
CONV_PAD = (CONV_WIDTH - 1) // 2
D_FF = 4 * D_MODEL
EPS = 1e-6

P_COLS = 7 * D_MODEL
OFF_G = 2 * D_MODEL
OFF_PWA = 3 * D_MODEL
OFF_PWB = 4 * D_MODEL
OFF_GA = 5 * D_MODEL
OFF_GB = 6 * D_MODEL
Z_COLS = 128

HALO = 16

_NT = (((1,), (1,)), ((), ()))
_TN = (((0,), (0,)), ((), ()))

_MIB = 1024 * 1024


def _cparams(sem, vmem_mib):
    return pltpu.CompilerParams(dimension_semantics=sem, vmem_limit_bytes=vmem_mib * _MIB)


def _sigmoid(x):
    return 1.0 / (1.0 + jnp.exp(-x))


def _log_sigmoid(x):
    return jnp.minimum(x, 0.0) - jnp.log(1.0 + jnp.exp(-jnp.abs(x)))


def _inproj_kernel(x_ref, g_ref, wa_ref, wb_ref, wz_ref, p_ref, z_ref, u_ref, *, rows, n_a):
    j = pl.program_id(1)

    @pl.when(j == 0)
    def _():
        def body(r, c):
            sl = pl.ds(pl.multiple_of(r * rows, rows), rows)
            x = x_ref[sl, :]
            ms = jnp.mean(x * x, axis=-1, keepdims=True)
            u = (x * lax.rsqrt(ms + EPS)) * g_ref[...]
            u_ref[sl, :] = u.astype(BF16)
            return c
        lax.fori_loop(0, x_ref.shape[0] // rows, body, 0)
        z_ref[...] = jnp.dot(u_ref[...], wz_ref[...], preferred_element_type=F32)

    @pl.when(j < n_a)
    def _():
        p_ref[...] = jnp.dot(u_ref[...], wa_ref[...], preferred_element_type=F32).astype(p_ref.dtype)

    @pl.when(j >= n_a)
    def _():
        p_ref[...] = jnp.dot(u_ref[...], wb_ref[...], preferred_element_type=F32).astype(p_ref.dtype)


def _inproj(x2d, g_pre, w_a, w_b, w_z, *, tm=1024, tn=1024):
    T = x2d.shape[0]
    n_a = w_a.shape[1] // tn
    grid = (T // tm, P_COLS // tn)
    return pl.pallas_call(
        functools.partial(_inproj_kernel, rows=128, n_a=n_a),
        grid=grid,
        in_specs=[
            pl.BlockSpec((tm, D_MODEL), lambda i, j: (i, 0)),
            pl.BlockSpec((1, D_MODEL), lambda i, j: (0, 0)),
            pl.BlockSpec((D_MODEL, tn), lambda i, j: (0, jnp.minimum(j, n_a - 1))),
            pl.BlockSpec((D_MODEL, tn), lambda i, j: (0, jnp.maximum(j - n_a, 0))),
            pl.BlockSpec((D_MODEL, Z_COLS), lambda i, j: (0, 0)),
        ],
        out_specs=[
            pl.BlockSpec((tm, tn), lambda i, j: (i, j)),
            pl.BlockSpec((tm, Z_COLS), lambda i, j: (i, 0)),
        ],
        out_shape=[
            jax.ShapeDtypeStruct((T, P_COLS), BF16),
            jax.ShapeDtypeStruct((T, Z_COLS), F32),
        ],
        scratch_shapes=[pltpu.VMEM((tm, D_MODEL), BF16)],
        compiler_params=_cparams(("parallel", "arbitrary"), 52),
        name="inproj",
    )(x2d, g_pre, w_a, w_b, w_z)


def _gla_kernel(qf_ref, kf_ref, vf_ref, zf_ref, qb_ref, kb_ref, vb_ref, zb_ref,
                waf_ref, baf_ref, wab_ref, bab_ref, of_ref, ob_ref,
                st_ref, la_ref, b_ref, qk_ref, d_ref, p_ref):
    @pl.when(pl.program_id(1) == 0)
    def _():
        st_ref[...] = jnp.zeros_like(st_ref)

    def log_decay(z_ref, w_ref, bias_ref):
        pre = jnp.dot(z_ref[...].astype(BF16), w_ref[...], preferred_element_type=F32) + bias_ref[...]
        return _log_sigmoid(pre) / GATE_TAU

    la_ref[0] = log_decay(zf_ref, waf_ref, baf_ref)
    la_ref[1] = log_decay(zb_ref, wab_ref, bab_ref)

    row3 = lax.broadcasted_iota(jnp.int32, (CHUNK, 3 * CHUNK), 0)
    col3 = lax.broadcasted_iota(jnp.int32, (CHUNK, 3 * CHUNK), 1) & (CHUNK - 1)
    row = lax.broadcasted_iota(jnp.int32, (CHUNK, CHUNK), 0)
    col = lax.broadcasted_iota(jnp.int32, (CHUNK, CHUNK), 1)
    dirs = (
        (qf_ref, kf_ref, vf_ref, of_ref, (col3 <= row3).astype(BF16), col <= row, CHUNK // 2, CHUNK - 1),
        (qb_ref, kb_ref, vb_ref, ob_ref, (col3 >= row3).astype(BF16), col >= row, CHUNK - 1 - CHUNK // 2, 0),
    )
    n_chunks = qf_ref.shape[0] // CHUNK
    chains = [(dr, hd) for dr in range(2) for hd in range(N_HEADS)]
    klanes = lambda hd: slice(hd * HEAD_DK, (hd + 1) * HEAD_DK)
    vlanes = lambda hd: slice(hd * HEAD_DV, (hd + 1) * HEAD_DV)

    def body(c, carry):
        rows = (pl.ds(pl.multiple_of(c * CHUNK, CHUNK), CHUNK),
                pl.ds(pl.multiple_of((n_chunks - 1 - c) * CHUNK, CHUNK), CHUNK))

        for dr in range(2):
            tri3, last_idx = dirs[dr][4], dirs[dr][7]
            la = la_ref[dr, rows[dr], :]
            h1 = la.astype(BF16)
            r1 = la - h1.astype(F32)
            h2 = r1.astype(BF16)
            h3 = (r1 - h2.astype(F32)).astype(BF16)
            b = jnp.dot(tri3, jnp.concatenate([h1, h2, h3], axis=0), preferred_element_type=F32)
            b_ref[dr] = b
            d_ref[dr, 0:1, :] = jnp.exp(b[last_idx:last_idx + 1, :])

        for dr, hd in chains:
            q_ref, k_ref, ref_idx, last_idx = dirs[dr][0], dirs[dr][1], dirs[dr][6], dirs[dr][7]
            kl = klanes(hd)
            b = b_ref[dr, :, kl]
            b_mid = b[ref_idx:ref_idx + 1, :]
            b_last = b[last_idx:last_idx + 1, :]
            q = q_ref[rows[dr], kl].astype(F32) * (HEAD_DK ** -0.5)
            k = k_ref[rows[dr], kl].astype(F32)
            qk_ref[dr, 0, :, kl] = (q * jnp.exp(b)).astype(BF16)
            qk_ref[dr, 1, :, kl] = (q * jnp.exp(b - b_mid)).astype(BF16)
            qk_ref[dr, 2, :, kl] = (k * jnp.exp(b_mid - b)).astype(BF16)
            qk_ref[dr, 3, :, kl] = (k * jnp.exp(b_last - b)).astype(BF16)

        for dr, hd in chains:
            o_ref = dirs[dr][3]
            st = st_ref[dr * N_HEADS + hd].astype(BF16)
            o_ref[rows[dr], vlanes(hd)] = lax.dot_general(
                qk_ref[dr, 0, :, klanes(hd)], st, _NT, preferred_element_type=F32)

        for dr, hd in chains:
            kl = klanes(hd)
            s = lax.dot_general(qk_ref[dr, 1, :, kl], qk_ref[dr, 2, :, kl], _NT, preferred_element_type=F32)
            p_ref[dr * N_HEADS + hd] = jnp.where(dirs[dr][5], s, 0.0).astype(BF16)

        for dr, hd in chains:
            v_ref = dirs[dr][2]
            ch = dr * N_HEADS + hd
            upd = lax.dot_general(v_ref[rows[dr], vlanes(hd)], qk_ref[dr, 3, :, klanes(hd)], _TN,
                                  preferred_element_type=F32)
            st_ref[ch] = d_ref[dr, 0:1, klanes(hd)] * st_ref[ch] + upd

        for dr, hd in chains:
            v_ref, o_ref = dirs[dr][2], dirs[dr][3]
            o_ref[rows[dr], vlanes(hd)] += jnp.dot(p_ref[dr * N_HEADS + hd], v_ref[rows[dr], vlanes(hd)],
                                                   preferred_element_type=F32)
        return carry

    lax.fori_loop(0, n_chunks, body, 0)


def _gla(P, Z, wa_f, ba_f, wa_b, ba_b, B, S, *, ts=512):
    T = B * S
    nb = S // ts
    fwd = lambda b, i: b * nb + i
    bwd = lambda b, i: b * nb + (nb - 1 - i)

    def stream_specs(rowmap):
        return [
            pl.BlockSpec((ts, D_K), lambda b, i: (rowmap(b, i), 0)),
            pl.BlockSpec((ts, D_K), lambda b, i: (rowmap(b, i), 1)),
            pl.BlockSpec((ts, D_V), lambda b, i: (rowmap(b, i), 1)),
            pl.BlockSpec((ts, Z_COLS), lambda b, i: (rowmap(b, i), 0)),
        ]

    w_spec = pl.BlockSpec((Z_COLS, D_K), lambda b, i: (0, 0))
    b_spec = pl.BlockSpec((1, D_K), lambda b, i: (0, 0))
    return pl.pallas_call(
        _gla_kernel,
        grid=(B, nb),
        in_specs=stream_specs(fwd) + stream_specs(bwd) + [w_spec, b_spec, w_spec, b_spec],
        out_specs=[
            pl.BlockSpec((ts, D_V), lambda b, i: (fwd(b, i), 0)),
            pl.BlockSpec((ts, D_V), lambda b, i: (bwd(b, i), 0)),
        ],
        out_shape=[jax.ShapeDtypeStruct((T, D_V), F32)] * 2,
        scratch_shapes=[
            pltpu.VMEM((2 * N_HEADS, HEAD_DV, HEAD_DK), F32),
            pltpu.VMEM((2, ts, D_K), F32),
            pltpu.VMEM((2, CHUNK, D_K), F32),
            pltpu.VMEM((2, 4, CHUNK, D_K), BF16),
            pltpu.VMEM((2, 8, D_K), F32),
            pltpu.VMEM((2 * N_HEADS, CHUNK, CHUNK), BF16),
        ],
        compiler_params=_cparams(("parallel", "arbitrary"), 56),
        name="gla",
    )(P, P, P, Z, P, P, P, Z, wa_f, ba_f, wa_b, ba_b)


def _conv_kernel(pa_ref, pb_ref, pa_prev, pb_prev, pa_next, pb_next,
                 cw_ref, cb_ref, lg_ref, lb_ref, w2_ref, out_ref,
                 glu_ref, dw_ref, act_ref, *, rb, lb):
    i = pl.program_id(1)
    last = pl.num_programs(1) - 1
    tt = pa_ref.shape[0]

    def glu(a, b):
        return a.astype(F32) * _sigmoid(b.astype(F32))

    n_win = tt // rb
    head = rb + HALO
    glu_ref[0, 0:HALO, :] = jnp.where(i > 0, glu(pa_prev[...], pb_prev[...]), 0.0)
    glu_ref[0, HALO:HALO + head, :] = glu(pa_ref[0:head, :], pb_ref[0:head, :])
    glu_ref[n_win - 1, 0:head, :] = glu(pa_ref[tt - head:tt, :], pb_ref[tt - head:tt, :])
    glu_ref[n_win - 1, head:head + HALO, :] = jnp.where(i < last, glu(pa_next[...], pb_next[...]), 0.0)

    def fill_body(m, c):
        src = pl.ds(pl.multiple_of(m * rb - HALO, HALO), rb + 2 * HALO)
        glu_ref[m] = glu(pa_ref[src, :], pb_ref[src, :])
        return c

    lax.fori_loop(1, n_win - 1, fill_body, 0)

    n_lb = D_MODEL // lb
    sub = 8
    shift = HALO - CONV_PAD

    def conv_body(it, c):
        m = it // n_lb
        lanes = pl.ds(pl.multiple_of((it % n_lb) * lb, lb), lb)
        acc = None
        for s in range(sub):
            part = None
            for q in range((CONV_WIDTH + shift) // sub + 1):
                j = sub * q + s - shift
                if 0 <= j < CONV_WIDTH:
                    term = cw_ref[j:j + 1, lanes] * glu_ref[m, pl.ds(sub * q, rb + sub), lanes]
                    part = term if part is None else part + term
            shifted = part[s:s + rb]
            acc = shifted if acc is None else acc + shifted
        dw_ref[pl.ds(pl.multiple_of(m * rb, rb), rb), lanes] = acc + cb_ref[:, lanes]
        return c

    lax.fori_loop(0, n_win * n_lb, conv_body, 0)

    ln_rows = 64

    def ln_body(r, c):
        sl = pl.ds(pl.multiple_of(r * ln_rows, ln_rows), ln_rows)
        x = dw_ref[sl, :]
        mu = jnp.mean(x, axis=-1, keepdims=True)
        xc = x - mu
        var = jnp.mean(xc * xc, axis=-1, keepdims=True)
        y = xc * lax.rsqrt(var + EPS) * lg_ref[...] + lb_ref[...]
        act_ref[sl, :] = (y * _sigmoid(y)).astype(BF16)
        return c

    lax.fori_loop(0, tt // ln_rows, ln_body, 0)
    out_ref[...] = jnp.dot(act_ref[...], w2_ref[...], preferred_element_type=F32).astype(out_ref.dtype)


def _conv_module(P, conv_w, conv_b, ln_g, ln_b, w_pw2, B, S, *, tt=512):
    T = B * S
    nt = S // tt
    hb = tt // HALO
    n_halo_blocks = T // HALO
    ca, cb = OFF_PWA // D_MODEL, OFF_PWB // D_MODEL
    row = lambda b, i: b * nt + i
    prev = lambda b, i: jnp.maximum(row(b, i) * hb - 1, 0)
    nxt = lambda b, i: jnp.minimum((row(b, i) + 1) * hb, n_halo_blocks - 1)
    vec = lambda n: pl.BlockSpec((n, D_MODEL), lambda b, i: (0, 0))
    rb = 64
    return pl.pallas_call(
        functools.partial(_conv_kernel, rb=rb, lb=128),
        grid=(B, nt),
        in_specs=[
            pl.BlockSpec((tt, D_MODEL), lambda b, i: (row(b, i), ca)),
            pl.BlockSpec((tt, D_MODEL), lambda b, i: (row(b, i), cb)),
            pl.BlockSpec((HALO, D_MODEL), lambda b, i: (prev(b, i), ca)),
            pl.BlockSpec((HALO, D_MODEL), lambda b, i: (prev(b, i), cb)),
            pl.BlockSpec((HALO, D_MODEL), lambda b, i: (nxt(b, i), ca)),
            pl.BlockSpec((HALO, D_MODEL), lambda b, i: (nxt(b, i), cb)),
            vec(CONV_WIDTH), vec(1), vec(1), vec(1),
            pl.BlockSpec((D_MODEL, D_MODEL), lambda b, i: (0, 0)),
        ],
        out_specs=pl.BlockSpec((tt, D_MODEL), lambda b, i: (row(b, i), 0)),
        out_shape=jax.ShapeDtypeStruct((T, D_MODEL), BF16),
        scratch_shapes=[
            pltpu.VMEM((tt // rb, rb + 2 * HALO, D_MODEL), F32),
            pltpu.VMEM((tt, D_MODEL), F32),
            pltpu.VMEM((tt, D_MODEL), BF16),
        ],
        compiler_params=_cparams(("parallel", "parallel"), 48),
        name="conv_module",
    )(P, P, P, P, P, P, conv_w, conv_b, ln_g, ln_b, w_pw2)


def _merge_kernel(of_ref, ob_ref, g_ref, ga_ref, gb_ref, cb_ref, x_ref, gn_ref, gpost_ref, wo_ref,
                  h_ref, m_ref, *, rows, norm_rows):
    tt = x_ref.shape[0]

    def merge_body(r, c):
        sl = pl.ds(pl.multiple_of(r * rows, rows), rows)
        o = of_ref[sl, :] + ob_ref[sl, :]
        heads = []
        for hd in range(N_HEADS):
            oh = o[:, hd * HEAD_DV:(hd + 1) * HEAD_DV]
            ms = jnp.mean(oh * oh, axis=-1, keepdims=True)
            heads.append(oh * lax.rsqrt(ms + EPS))
        on = jnp.concatenate(heads, axis=-1) * gn_ref[...]
        g = g_ref[sl, :].astype(F32)
        out_a = on * (g * _sigmoid(g))
        merged = (_sigmoid(ga_ref[sl, :].astype(F32)) * out_a
                  + _sigmoid(gb_ref[sl, :].astype(F32)) * cb_ref[sl, :].astype(F32))
        m_ref[sl, :] = merged.astype(BF16)
        return c

    lax.fori_loop(0, tt // rows, merge_body, 0)
    h_ref[...] = jnp.dot(m_ref[...], wo_ref[...], preferred_element_type=F32)

    def norm_body(r, c):
        sl = pl.ds(pl.multiple_of(r * norm_rows, norm_rows), norm_rows)
        y = h_ref[sl, :]
        ms = jnp.mean(y * y, axis=-1, keepdims=True)
        h_ref[sl, :] = x_ref[sl, :] + (y * lax.rsqrt(ms + EPS)) * gpost_ref[...]
        return c

    lax.fori_loop(0, tt // norm_rows, norm_body, 0)


def _merge(o_f, o_b, P, out_b, x2d, gla_norm, g_post, w_out, *, tt=256):
    T = x2d.shape[0]
    tile = lambda c: pl.BlockSpec((tt, D_MODEL), lambda i: (i, c))
    vec = pl.BlockSpec((1, D_MODEL), lambda i: (0, 0))
    return pl.pallas_call(
        functools.partial(_merge_kernel, rows=16, norm_rows=64),
        grid=(T // tt,),
        in_specs=[
            tile(0), tile(0),
            tile(OFF_G // D_MODEL), tile(OFF_GA // D_MODEL), tile(OFF_GB // D_MODEL),
            tile(0), tile(0), vec, vec,
            pl.BlockSpec((D_MODEL, D_MODEL), lambda i: (0, 0)),
        ],
        out_specs=tile(0),
        out_shape=jax.ShapeDtypeStruct((T, D_MODEL), F32),
        scratch_shapes=[pltpu.VMEM((tt, D_MODEL), BF16)],
        compiler_params=_cparams(("parallel",), 48),
        name="merge_outproj",
    )(o_f, o_b, P, P, P, out_b, x2d, gla_norm, g_post, w_out)


def _ffn_kernel(h_ref, gpre_ref, gpost_ref, w1_ref, w2_ref, y_ref, u_ref, acc_ref, *, rows):
    f = pl.program_id(1)
    tm = h_ref.shape[0]

    @pl.when(f == 0)
    def _():
        def body(r, c):
            sl = pl.ds(pl.multiple_of(r * rows, rows), rows)
            x = h_ref[sl, :]
            ms = jnp.mean(x * x, axis=-1, keepdims=True)
            u_ref[sl, :] = ((x * lax.rsqrt(ms + EPS)) * gpre_ref[...]).astype(BF16)
            return c
        lax.fori_loop(0, tm // rows, body, 0)
        acc_ref[...] = jnp.zeros_like(acc_ref)

    a = jnp.dot(u_ref[...], w1_ref[...], preferred_element_type=F32)
    a = jnp.maximum(a, 0.0)
    a = (a * a).astype(BF16)
    acc_ref[...] += jnp.dot(a, w2_ref[...], preferred_element_type=F32)

    @pl.when(f == pl.num_programs(1) - 1)
    def _():
        def body(r, c):
            sl = pl.ds(pl.multiple_of(r * rows, rows), rows)
            y = acc_ref[sl, :]
            ms = jnp.mean(y * y, axis=-1, keepdims=True)
            y_ref[sl, :] = h_ref[sl, :] + (y * lax.rsqrt(ms + EPS)) * gpost_ref[...]
            return c
        lax.fori_loop(0, tm // rows, body, 0)


def _ffn(h2d, g_pre, g_post, w1, w2, *, tm=512, tf=1024):
    T = h2d.shape[0]
    vec = pl.BlockSpec((1, D_MODEL), lambda i, f: (0, 0))
    return pl.pallas_call(
        functools.partial(_ffn_kernel, rows=128),
        grid=(T // tm, D_FF // tf),
        in_specs=[
            pl.BlockSpec((tm, D_MODEL), lambda i, f: (i, 0)),
            vec, vec,
            pl.BlockSpec((D_MODEL, tf), lambda i, f: (0, f)),
            pl.BlockSpec((tf, D_MODEL), lambda i, f: (f, 0)),
        ],
        out_specs=pl.BlockSpec((tm, D_MODEL), lambda i, f: (i, 0)),
        out_shape=jax.ShapeDtypeStruct((T, D_MODEL), F32),
        scratch_shapes=[pltpu.VMEM((tm, D_MODEL), BF16), pltpu.VMEM((tm, D_MODEL), F32)],
        compiler_params=_cparams(("parallel", "arbitrary"), 48),
        name="ffn",
    )(h2d, g_pre, g_post, w1, w2)


def _encoder_layer(x, wts):
    B, S, _ = x.shape
    x2d = x.reshape(B * S, D_MODEL)
    P, Z = _inproj(x2d, wts["g_mix_pre"], wts["w_a"], wts["w_b"], wts["w_z"])
    o_f, o_b = _gla(P, Z, wts["wa_f"], wts["ba_f"], wts["wa_b"], wts["ba_b"], B, S)
    out_b = _conv_module(P, wts["conv_w"], wts["conv_b"], wts["ln_g"], wts["ln_b"], wts["w_pw2"], B, S)
    h = _merge(o_f, o_b, P, out_b, x2d, wts["gla_norm"], wts["g_mix_post"], wts["w_out"])
    y = _ffn(h, wts["g_ffn_pre"], wts["g_ffn_post"], wts["w_ff1"], wts["w_ff2"])
    return y.reshape(B, S, D_MODEL)


def kernel(x_prompt, x_sample, norm_mix_pre, norm_mix_post, norm_ffn_pre, norm_ffn_post, w_in, w_a_fwd, b_a_fwd, w_a_bwd, b_a_bwd, gla_norm, conv_w, conv_b, conv_ln_g, conv_ln_b, w_pw2, w_out, w_ff1, w_ff2):
    z0 = 2 * D_K + 2 * D_V
    z1 = z0 + 2 * GATE_RANK
    row = lambda v: v.reshape(1, -1).astype(F32)
    w_z = jnp.zeros((D_MODEL, Z_COLS), F32).at[:, :2 * GATE_RANK].set(w_in[:, z0:z1])
    wa_f = jnp.zeros((Z_COLS, D_K), F32).at[:GATE_RANK].set(w_a_fwd)
    wa_b = jnp.zeros((Z_COLS, D_K), F32).at[GATE_RANK:2 * GATE_RANK].set(w_a_bwd)
    wts = {
        "g_mix_pre": row(norm_mix_pre), "g_mix_post": row(norm_mix_post),
        "g_ffn_pre": row(norm_ffn_pre), "g_ffn_post": row(norm_ffn_post),
        "w_a": w_in[:, :z0].astype(BF16), "w_b": w_in[:, z1:].astype(BF16),
        "w_z": w_z.astype(BF16),
        "wa_f": wa_f.astype(BF16), "ba_f": row(b_a_fwd),
        "wa_b": wa_b.astype(BF16), "ba_b": row(b_a_bwd),
        "gla_norm": row(gla_norm),
        "conv_w": conv_w, "conv_b": row(conv_b),
        "ln_g": row(conv_ln_g), "ln_b": row(conv_ln_b),
        "w_pw2": w_pw2.astype(BF16), "w_out": w_out.astype(BF16),
        "w_ff1": w_ff1.astype(BF16), "w_ff2": w_ff2.astype(BF16),
    }
    return (_encoder_layer(x_prompt, wts), _encoder_layer(x_sample, wts))
```

```python
import functools

import jax
import jax.numpy as jnp
from jax import lax
from jax.experimental import pallas as pl
from jax.experimental.pallas import tpu as pltpu

F32 = jnp.float32
BF16 = jnp.bfloat16

D_MODEL = 2048
N_HEADS = 4
D_K = D_MODEL // 2
D_V = D_MODEL
HEAD_DK = D_K // N_HEADS
HEAD_DV = D_V // N_HEADS
GATE_RANK = 16
GATE_TAU = 16.0
CHUNK = 64
CONV_WIDTH = 31
CONV_PAD = (CONV_WIDTH - 1) // 2
D_FF = 4 * D_MODEL
EPS = 1e-6

P_COLS = 7 * D_MODEL
OFF_G = 2 * D_MODEL
OFF_PWA = 3 * D_MODEL
OFF_PWB = 4 * D_MODEL
OFF_GA = 5 * D_MODEL
OFF_GB = 6 * D_MODEL
Z_COLS = 128

HALO = 16

_NT = (((1,), (1,)), ((), ()))
_TN = (((0,), (0,)), ((), ()))

_MIB = 1024 * 1024


def _cparams(sem, vmem_mib):
    return pltpu.CompilerParams(dimension_semantics=sem, vmem_limit_bytes=vmem_mib * _MIB)


def _sigmoid(x):
    return 1.0 / (1.0 + jnp.exp(-x))


def _log_sigmoid(x):
    return jnp.minimum(x, 0.0) - jnp.log(1.0 + jnp.exp(-jnp.abs(x)))


def _inproj_kernel(x_ref, g_ref, wa_ref, wb_ref, wz_ref, p_ref, z_ref, u_ref, *, rows, n_a):
    j = pl.program_id(1)

    @pl.when(j == 0)
    def _():
        def body(r, c):
            sl = pl.ds(pl.multiple_of(r * rows, rows), rows)
            x = x_ref[sl, :]
            ms = jnp.mean(x * x, axis=-1, keepdims=True)
            u = (x * lax.rsqrt(ms + EPS)) * g_ref[...]
            u_ref[sl, :] = u.astype(BF16)
            return c
        lax.fori_loop(0, x_ref.shape[0] // rows, body, 0)
        z_ref[...] = jnp.dot(u_ref[...], wz_ref[...], preferred_element_type=F32)

    w = jnp.where(j < n_a, wa_ref[...], wb_ref[...])
    p_ref[...] = jnp.dot(u_ref[...], w, preferred_element_type=F32).astype(p_ref.dtype)


def _inproj(x2d, g_pre, w_a, w_b, w_z, *, tm=1024, tn=1024):
    T = x2d.shape[0]
    n_a = w_a.shape[1] // tn
    grid = (T // tm, P_COLS // tn)
    return pl.pallas_call(
        functools.partial(_inproj_kernel, rows=128, n_a=n_a),
        grid=grid,
        in_specs=[
            pl.BlockSpec((tm, D_MODEL), lambda i, j: (i, 0)),
            pl.BlockSpec((1, D_MODEL), lambda i, j: (0, 0)),
            pl.BlockSpec((D_MODEL, tn), lambda i, j: (0, jnp.minimum(j, n_a - 1))),
            pl.BlockSpec((D_MODEL, tn), lambda i, j: (0, jnp.maximum(j - n_a, 0))),
            pl.BlockSpec((D_MODEL, Z_COLS), lambda i, j: (0, 0)),
        ],
        out_specs=[
            pl.BlockSpec((tm, tn), lambda i, j: (i, j)),
            pl.BlockSpec((tm, Z_COLS), lambda i, j: (i, 0)),
        ],
        out_shape=[
            jax.ShapeDtypeStruct((T, P_COLS), BF16),
            jax.ShapeDtypeStruct((T, Z_COLS), F32),
        ],
        scratch_shapes=[pltpu.VMEM((tm, D_MODEL), BF16)],
        compiler_params=_cparams(("parallel", "arbitrary"), 52),
        name="inproj",
    )(x2d, g_pre, w_a, w_b, w_z)


def _gla_kernel(qf_ref, kf_ref, vf_ref, zf_ref, qb_ref, kb_ref, vb_ref, zb_ref,
                waf_ref, baf_ref, wab_ref, bab_ref, of_ref, ob_ref,
                st_ref, la_ref, b_ref, qk_ref, d_ref, p_ref):
    @pl.when(pl.program_id(1) == 0)
    def _():
        st_ref[...] = jnp.zeros_like(st_ref)

    def log_decay(z_ref, w_ref, bias_ref):
        pre = jnp.dot(z_ref[...].astype(BF16), w_ref[...], preferred_element_type=F32) + bias_ref[...]
        return _log_sigmoid(pre) / GATE_TAU

    la_ref[0] = log_decay(zf_ref, waf_ref, baf_ref)
    la_ref[1] = log_decay(zb_ref, wab_ref, bab_ref)

    row3 = lax.broadcasted_iota(jnp.int32, (CHUNK, 3 * CHUNK), 0)
    col3 = lax.broadcasted_iota(jnp.int32, (CHUNK, 3 * CHUNK), 1) & (CHUNK - 1)
    row = lax.broadcasted_iota(jnp.int32, (CHUNK, CHUNK), 0)
    col = lax.broadcasted_iota(jnp.int32, (CHUNK, CHUNK), 1)
    dirs = (
        (qf_ref, kf_ref, vf_ref, of_ref, (col3 <= row3).astype(BF16), col <= row, CHUNK // 2, CHUNK - 1),
        (qb_ref, kb_ref, vb_ref, ob_ref, (col3 >= row3).astype(BF16), col >= row, CHUNK - 1 - CHUNK // 2, 0),
    )
    n_chunks = qf_ref.shape[0] // CHUNK
    chains = [(dr, hd) for dr in range(2) for hd in range(N_HEADS)]
    klanes = lambda hd: slice(hd * HEAD_DK, (hd + 1) * HEAD_DK)
    vlanes = lambda hd: slice(hd * HEAD_DV, (hd + 1) * HEAD_DV)

    def body(c, carry):
        rows = (pl.ds(pl.multiple_of(c * CHUNK, CHUNK), CHUNK),
                pl.ds(pl.multiple_of((n_chunks - 1 - c) * CHUNK, CHUNK), CHUNK))

        for dr in range(2):
            tri3, last_idx = dirs[dr][4], dirs[dr][7]
            la = la_ref[dr, rows[dr], :]
            h1 = la.astype(BF16)
            r1 = la - h1.astype(F32)
            h2 = r1.astype(BF16)
            h3 = (r1 - h2.astype(F32)).astype(BF16)
            b = jnp.dot(tri3, jnp.concatenate([h1, h2, h3], axis=0), preferred_element_type=F32)
            b_ref[dr] = b
            d_ref[dr, 0:1, :] = jnp.exp(b[last_idx:last_idx + 1, :])

        for dr, hd in chains:
            q_ref, k_ref, ref_idx, last_idx = dirs[dr][0], dirs[dr][1], dirs[dr][6], dirs[dr][7]
            kl = klanes(hd)
            b = b_ref[dr, :, kl]
            b_mid = b[ref_idx:ref_idx + 1, :]
            b_last = b[last_idx:last_idx + 1, :]
            q = q_ref[rows[dr], kl].astype(F32) * (HEAD_DK ** -0.5)
            k = k_ref[rows[dr], kl].astype(F32)
            qk_ref[dr, 0, :, kl] = (q * jnp.exp(b)).astype(BF16)
            qk_ref[dr, 1, :, kl] = (q * jnp.exp(b - b_mid)).astype(BF16)
            qk_ref[dr, 2, :, kl] = (k * jnp.exp(b_mid - b)).astype(BF16)
            qk_ref[dr, 3, :, kl] = (k * jnp.exp(b_last - b)).astype(BF16)

        for dr, hd in chains:
            o_ref = dirs[dr][3]
            st = st_ref[dr * N_HEADS + hd].astype(BF16)
            o_ref[rows[dr], vlanes(hd)] = lax.dot_general(
                qk_ref[dr, 0, :, klanes(hd)], st, _NT, preferred_element_type=F32)

        for dr, hd in chains:
            kl = klanes(hd)
            s = lax.dot_general(qk_ref[dr, 1, :, kl], qk_ref[dr, 2, :, kl], _NT, preferred_element_type=F32)
            p_ref[dr * N_HEADS + hd] = jnp.where(dirs[dr][5], s, 0.0).astype(BF16)

        for dr, hd in chains:
            v_ref = dirs[dr][2]
            ch = dr * N_HEADS + hd
            upd = lax.dot_general(v_ref[rows[dr], vlanes(hd)], qk_ref[dr, 3, :, klanes(hd)], _TN,
                                  preferred_element_type=F32)
            st_ref[ch] = d_ref[dr, 0:1, klanes(hd)] * st_ref[ch] + upd

        for dr, hd in chains:
            v_ref, o_ref = dirs[dr][2], dirs[dr][3]
            o_ref[rows[dr], vlanes(hd)] += jnp.dot(p_ref[dr * N_HEADS + hd], v_ref[rows[dr], vlanes(hd)],
                                                   preferred_element_type=F32)
        return carry

    lax.fori_loop(0, n_chunks, body, 0)


def _gla(P, Z, wa_f, ba_f, wa_b, ba_b, B, S, *, ts=512):
    T = B * S
    nb = S // ts
    fwd = lambda b, i: b * nb + i
    bwd = lambda b, i: b * nb + (nb - 1 - i)

    def stream_specs(rowmap):
        return [
            pl.BlockSpec((ts, D_K), lambda b, i: (rowmap(b, i), 0)),
            pl.BlockSpec((ts, D_K), lambda b, i: (rowmap(b, i), 1)),
            pl.BlockSpec((ts, D_V), lambda b, i: (rowmap(b, i), 1)),
            pl.BlockSpec((ts, Z_COLS), lambda b, i: (rowmap(b, i), 0)),
        ]

    w_spec = pl.BlockSpec((Z_COLS, D_K), lambda b, i: (0, 0))
    b_spec = pl.BlockSpec((1, D_K), lambda b, i: (0, 0))
    return pl.pallas_call(
        _gla_kernel,
        grid=(B, nb),
        in_specs=stream_specs(fwd) + stream_specs(bwd) + [w_spec, b_spec, w_spec, b_spec],
        out_specs=[
            pl.BlockSpec((ts, D_V), lambda b, i: (fwd(b, i), 0)),
            pl.BlockSpec((ts, D_V), lambda b, i: (bwd(b, i), 0)),
        ],
        out_shape=[jax.ShapeDtypeStruct((T, D_V), F32)] * 2,
        scratch_shapes=[
            pltpu.VMEM((2 * N_HEADS, HEAD_DV, HEAD_DK), F32),
            pltpu.VMEM((2, ts, D_K), F32),
            pltpu.VMEM((2, CHUNK, D_K), F32),
            pltpu.VMEM((2, 4, CHUNK, D_K), BF16),
            pltpu.VMEM((2, 8, D_K), F32),
            pltpu.VMEM((2 * N_HEADS, CHUNK, CHUNK), BF16),
        ],
        compiler_params=_cparams(("parallel", "arbitrary"), 56),
        name="gla",
    )(P, P, P, Z, P, P, P, Z, wa_f, ba_f, wa_b, ba_b)


def _conv_kernel(pa_ref, pb_ref, pa_prev, pb_prev, pa_next, pb_next,
                 cw_ref, cb_ref, lg_ref, lb_ref, w2_ref, out_ref,
                 glu_ref, dw_ref, act_ref, *, rb, lb, nt):
    s = pl.program_id(0)
    tile = jnp.minimum(s, pl.num_programs(0) - 2)
    i = tile % nt
    slot = s % 2
    tt = pa_ref.shape[0]

    @pl.when(s == 0)
    def _():
        act_ref[1] = jnp.zeros(act_ref.shape[1:], act_ref.dtype)

    def glu(a, b):
        return a.astype(F32) * _sigmoid(b.astype(F32))

    n_win = tt // rb
    head = rb + HALO
    glu_ref[0, 0:HALO, :] = jnp.where(i > 0, glu(pa_prev[...], pb_prev[...]), 0.0)
    glu_ref[0, HALO:HALO + head, :] = glu(pa_ref[0:head, :], pb_ref[0:head, :])
    glu_ref[n_win - 1, 0:head, :] = glu(pa_ref[tt - head:tt, :], pb_ref[tt - head:tt, :])
    glu_ref[n_win - 1, head:head + HALO, :] = jnp.where(i < nt - 1, glu(pa_next[...], pb_next[...]), 0.0)

    def fill_body(m, c):
        src = pl.ds(pl.multiple_of(m * rb - HALO, HALO), rb + 2 * HALO)
        glu_ref[m] = glu(pa_ref[src, :], pb_ref[src, :])
        return c

    lax.fori_loop(1, n_win - 1, fill_body, 0)

    sub = 8
    shift = HALO - CONV_PAD
    n_col = D_MODEL // n_win

    for m in range(n_win):
        cols = slice(m * n_col, (m + 1) * n_col)
        out_ref[:, cols] = jnp.dot(act_ref[1 - slot], w2_ref[:, cols],
                                   preferred_element_type=F32).astype(out_ref.dtype)
        out_rows = slice(m * rb, (m + 1) * rb)
        for l0 in range(0, D_MODEL, lb):
            lanes = slice(l0, l0 + lb)
            acc = None
            for ph in range(sub):
                part = None
                for q in range((CONV_WIDTH + shift) // sub + 1):
                    j = sub * q + ph - shift
                    if 0 <= j < CONV_WIDTH:
                        term = cw_ref[j:j + 1, lanes] * glu_ref[m, pl.ds(sub * q, rb + sub), lanes]
                        part = term if part is None else part + term
                shifted = part[ph:ph + rb]
                acc = shifted if acc is None else acc + shifted
            dw_ref[out_rows, lanes] = acc + cb_ref[:, lanes]

    ln_rows = 64

    def ln_body(r, c):
        sl = pl.ds(pl.multiple_of(r * ln_rows, ln_rows), ln_rows)
        x = dw_ref[sl, :]
        mu = jnp.mean(x, axis=-1, keepdims=True)
        xc = x - mu
        var = jnp.mean(xc * xc, axis=-1, keepdims=True)
        y = xc * lax.rsqrt(var + EPS) * lg_ref[...] + lb_ref[...]
        act_ref[slot, sl, :] = (y * _sigmoid(y)).astype(BF16)
        return c

    lax.fori_loop(0, tt // ln_rows, ln_body, 0)


def _conv_module(P, conv_w, conv_b, ln_g, ln_b, w_pw2, B, S, *, tt=512):
    T = B * S
    nt = S // tt
    n_tiles = B * nt
    hb = tt // HALO
    n_halo_blocks = T // HALO
    ca, cb = OFF_PWA // D_MODEL, OFF_PWB // D_MODEL
    tile = lambda s: jnp.minimum(s, n_tiles - 1)
    prev = lambda s: jnp.maximum(tile(s) * hb - 1, 0)
    nxt = lambda s: jnp.minimum((tile(s) + 1) * hb, n_halo_blocks - 1)
    vec = lambda n: pl.BlockSpec((n, D_MODEL), lambda s: (0, 0))
    rb = 64
    return pl.pallas_call(
        functools.partial(_conv_kernel, rb=rb, lb=128, nt=nt),
        grid=(n_tiles + 1,),
        in_specs=[
            pl.BlockSpec((tt, D_MODEL), lambda s: (tile(s), ca)),
            pl.BlockSpec((tt, D_MODEL), lambda s: (tile(s), cb)),
            pl.BlockSpec((HALO, D_MODEL), lambda s: (prev(s), ca)),
            pl.BlockSpec((HALO, D_MODEL), lambda s: (prev(s), cb)),
            pl.BlockSpec((HALO, D_MODEL), lambda s: (nxt(s), ca)),
            pl.BlockSpec((HALO, D_MODEL), lambda s: (nxt(s), cb)),
            vec(CONV_WIDTH), vec(1), vec(1), vec(1),
            pl.BlockSpec((D_MODEL, D_MODEL), lambda s: (0, 0)),
        ],
        out_specs=pl.BlockSpec((tt, D_MODEL), lambda s: (jnp.maximum(s - 1, 0), 0)),
        out_shape=jax.ShapeDtypeStruct((T, D_MODEL), BF16),
        scratch_shapes=[
            pltpu.VMEM((tt // rb, rb + 2 * HALO, D_MODEL), F32),
            pltpu.VMEM((tt, D_MODEL), F32),
            pltpu.VMEM((2, tt, D_MODEL), BF16),
        ],
        compiler_params=_cparams(("arbitrary",), 52),
        name="conv_module",
    )(P, P, P, P, P, P, conv_w, conv_b, ln_g, ln_b, w_pw2)


def _merge_kernel(of_ref, ob_ref, g_ref, ga_ref, gb_ref, cb_ref, x_ref, gn_ref, gpost_ref, wo_ref,
                  h_ref, m_ref, *, rows, norm_rows):
    s = pl.program_id(0)
    slot = s % 2
    tt = x_ref.shape[0]
    n_blk = tt // rows
    n_col = D_MODEL // n_blk

    @pl.when(s == 0)
    def _():
        m_ref[1] = jnp.zeros(m_ref.shape[1:], m_ref.dtype)

    for r in range(n_blk):
        cols = slice(r * n_col, (r + 1) * n_col)
        h_ref[:, cols] = jnp.dot(m_ref[1 - slot], wo_ref[:, cols], preferred_element_type=F32)
        sl = slice(r * rows, (r + 1) * rows)
        o = of_ref[sl, :] + ob_ref[sl, :]
        heads = []
        for hd in range(N_HEADS):
            oh = o[:, hd * HEAD_DV:(hd + 1) * HEAD_DV]
            ms = jnp.mean(oh * oh, axis=-1, keepdims=True)
            heads.append(oh * lax.rsqrt(ms + EPS))
        on = jnp.concatenate(heads, axis=-1) * gn_ref[...]
        g = g_ref[sl, :].astype(F32)
        out_a = on * (g * _sigmoid(g))
        merged = (_sigmoid(ga_ref[sl, :].astype(F32)) * out_a
                  + _sigmoid(gb_ref[sl, :].astype(F32)) * cb_ref[sl, :].astype(F32))
        m_ref[slot, sl, :] = merged.astype(BF16)

    def norm_body(r, c):
        sl = pl.ds(pl.multiple_of(r * norm_rows, norm_rows), norm_rows)
        y = h_ref[sl, :]
        ms = jnp.mean(y * y, axis=-1, keepdims=True)
        h_ref[sl, :] = x_ref[sl, :] + (y * lax.rsqrt(ms + EPS)) * gpost_ref[...]
        return c

    lax.fori_loop(0, tt // norm_rows, norm_body, 0)


def _merge(o_f, o_b, P, out_b, x2d, gla_norm, g_post, w_out, *, tt=256):
    T = x2d.shape[0]
    n_tiles = T // tt
    cur = lambda c: pl.BlockSpec((tt, D_MODEL), lambda s: (jnp.minimum(s, n_tiles - 1), c))
    lag = pl.BlockSpec((tt, D_MODEL), lambda s: (jnp.maximum(s - 1, 0), 0))
    vec = pl.BlockSpec((1, D_MODEL), lambda s: (0, 0))
    return pl.pallas_call(
        functools.partial(_merge_kernel, rows=32, norm_rows=64),
        grid=(n_tiles + 1,),
        in_specs=[
            cur(0), cur(0),
            cur(OFF_G // D_MODEL), cur(OFF_GA // D_MODEL), cur(OFF_GB // D_MODEL),
            cur(0), lag, vec, vec,
            pl.BlockSpec((D_MODEL, D_MODEL), lambda s: (0, 0)),
        ],
        out_specs=lag,
        out_shape=jax.ShapeDtypeStruct((T, D_MODEL), F32),
        scratch_shapes=[pltpu.VMEM((2, tt, D_MODEL), BF16)],
        compiler_params=_cparams(("arbitrary",), 48),
        name="merge_outproj",
    )(o_f, o_b, P, P, P, out_b, x2d, gla_norm, g_post, w_out)


def _ffn_kernel(h_ref, gpre_ref, gpost_ref, w1_ref, w2_ref, y_ref, u_ref, acc_ref, *, rows):
    f = pl.program_id(1)
    tm = h_ref.shape[0]

    @pl.when(f == 0)
    def _():
        def body(r, c):
            sl = pl.ds(pl.multiple_of(r * rows, rows), rows)
            x = h_ref[sl, :]
            ms = jnp.mean(x * x, axis=-1, keepdims=True)
            u_ref[sl, :] = ((x * lax.rsqrt(ms + EPS)) * gpre_ref[...]).astype(BF16)
            return c
        lax.fori_loop(0, tm // rows, body, 0)
        acc_ref[...] = jnp.zeros_like(acc_ref)

    a = jnp.dot(u_ref[...], w1_ref[...], preferred_element_type=F32)
    a = jnp.maximum(a, 0.0)
    a = (a * a).astype(BF16)
    acc_ref[...] += jnp.dot(a, w2_ref[...], preferred_element_type=F32)

    @pl.when(f == pl.num_programs(1) - 1)
    def _():
        def body(r, c):
            sl = pl.ds(pl.multiple_of(r * rows, rows), rows)
            y = acc_ref[sl, :]
            ms = jnp.mean(y * y, axis=-1, keepdims=True)
            y_ref[sl, :] = h_ref[sl, :] + (y * lax.rsqrt(ms + EPS)) * gpost_ref[...]
            return c
        lax.fori_loop(0, tm // rows, body, 0)


def _ffn(h2d, g_pre, g_post, w1, w2, *, tm=512, tf=1024):
    T = h2d.shape[0]
    vec = pl.BlockSpec((1, D_MODEL), lambda i, f: (0, 0))
    return pl.pallas_call(
        functools.partial(_ffn_kernel, rows=128),
        grid=(T // tm, D_FF // tf),
        in_specs=[
            pl.BlockSpec((tm, D_MODEL), lambda i, f: (i, 0)),
            vec, vec,
            pl.BlockSpec((D_MODEL, tf), lambda i, f: (0, f)),
            pl.BlockSpec((tf, D_MODEL), lambda i, f: (f, 0)),
        ],
        out_specs=pl.BlockSpec((tm, D_MODEL), lambda i, f: (i, 0)),
        out_shape=jax.ShapeDtypeStruct((T, D_MODEL), F32),
        scratch_shapes=[pltpu.VMEM((tm, D_MODEL), BF16), pltpu.VMEM((tm, D_MODEL), F32)],
        compiler_params=_cparams(("parallel", "arbitrary"), 48),
        name="ffn",
    )(h2d, g_pre, g_post, w1, w2)


def _encoder_layer(x, wts):
    B, S, _ = x.shape
    x2d = x.reshape(B * S, D_MODEL)
    P, Z = _inproj(x2d, wts["g_mix_pre"], wts["w_a"], wts["w_b"], wts["w_z"])
    o_f, o_b = _gla(P, Z, wts["wa_f"], wts["ba_f"], wts["wa_b"], wts["ba_b"], B, S)
    out_b = _conv_module(P, wts["conv_w"], wts["conv_b"], wts["ln_g"], wts["ln_b"], wts["w_pw2"], B, S)
    h = _merge(o_f, o_b, P, out_b, x2d, wts["gla_norm"], wts["g_mix_post"], wts["w_out"])
    y = _ffn(h, wts["g_ffn_pre"], wts["g_ffn_post"], wts["w_ff1"], wts["w_ff2"])
    return y.reshape(B, S, D_MODEL)


def kernel(x_prompt, x_sample, norm_mix_pre, norm_mix_post, norm_ffn_pre, norm_ffn_post, w_in, w_a_fwd, b_a_fwd, w_a_bwd, b_a_bwd, gla_norm, conv_w, conv_b, conv_ln_g, conv_ln_b, w_pw2, w_out, w_ff1, w_ff2):
    z0 = 2 * D_K + 2 * D_V
    z1 = z0 + 2 * GATE_RANK
    row = lambda v: v.reshape(1, -1).astype(F32)
    w_z = jnp.zeros((D_MODEL, Z_COLS), F32).at[:, :2 * GATE_RANK].set(w_in[:, z0:z1])
    wa_f = jnp.zeros((Z_COLS, D_K), F32).at[:GATE_RANK].set(w_a_fwd)
    wa_b = jnp.zeros((Z_COLS, D_K), F32).at[GATE_RANK:2 * GATE_RANK].set(w_a_bwd)
    wts = {
        "g_mix_pre": row(norm_mix_pre), "g_mix_post": row(norm_mix_post),
        "g_ffn_pre": row(norm_ffn_pre), "g_ffn_post": row(norm_ffn_post),
        "w_a": w_in[:, :z0].astype(BF16), "w_b": w_in[:, z1:].astype(BF16),
        "w_z": w_z.astype(BF16),
        "wa_f": wa_f.astype(BF16), "ba_f": row(b_a_fwd),
        "wa_b": wa_b.astype(BF16), "ba_b": row(b_a_bwd),
        "gla_norm": row(gla_norm),
        "conv_w": conv_w, "conv_b": row(conv_b),
        "ln_g": row(conv_ln_g), "ln_b": row(conv_ln_b),
        "w_pw2": w_pw2.astype(BF16), "w_out": w_out.astype(BF16),
        "w_ff1": w_ff1.astype(BF16), "w_ff2": w_ff2.astype(BF16),
    }
    return (_encoder_layer(x_prompt, wts), _encoder_layer(x_sample, wts))
```

```python
import functools

import jax
import jax.numpy as jnp
from jax import lax
from jax.experimental import pallas as pl
from jax.experimental.pallas import tpu as pltpu

F32 = jnp.float32
BF16 = jnp.bfloat16

D_MODEL = 2048
N_HEADS = 4
D_K = D_MODEL // 2
D_V = D_MODEL
HEAD_DK = D_K // N_HEADS
HEAD_DV = D_V // N_HEADS
GATE_RANK = 16
GATE_TAU = 16.0
CHUNK = 64
CONV_WIDTH = 31
CONV_PAD = (CONV_WIDTH - 1) // 2
D_FF = 4 * D_MODEL
EPS = 1e-6

P_COLS = 7 * D_MODEL
OFF_G = 2 * D_MODEL
OFF_PWA = 3 * D_MODEL
OFF_PWB = 4 * D_MODEL
OFF_GA = 5 * D_MODEL
OFF_GB = 6 * D_MODEL
Z_COLS = 128

HALO = 16
GLA_BLOCK = 2 * CHUNK

_NT = (((1,), (1,)), ((), ()))
_TN = (((0,), (0,)), ((), ()))

_MIB = 1024 * 1024


def _cparams(sem, vmem_mib):
    return pltpu.CompilerParams(dimension_semantics=sem, vmem_limit_bytes=vmem_mib * _MIB)


def _sigmoid(x):
    return 1.0 / (1.0 + jnp.exp(-x))


def _log_sigmoid(x):
    return jnp.minimum(x, 0.0) - jnp.log(1.0 + jnp.exp(-jnp.abs(x)))


def _inproj_kernel(x_ref, g_ref, wa_ref, wb_ref, wz_ref, p_ref, z_ref, u_ref, *, rows, n_a):
    j = pl.program_id(1)

    @pl.when(j == 0)
    def _():
        def body(r, c):
            sl = pl.ds(pl.multiple_of(r * rows, rows), rows)
            x = x_ref[sl, :]
            ms = jnp.mean(x * x, axis=-1, keepdims=True)
            u = (x * lax.rsqrt(ms + EPS)) * g_ref[...]
            u_ref[sl, :] = u.astype(BF16)
            return c
        lax.fori_loop(0, x_ref.shape[0] // rows, body, 0)
        z_ref[...] = jnp.dot(u_ref[...], wz_ref[...], preferred_element_type=F32)

    w = jnp.where(j < n_a, wa_ref[...], wb_ref[...])
    p_ref[...] = jnp.dot(u_ref[...], w, preferred_element_type=F32).astype(p_ref.dtype)


def _inproj(x2d, g_pre, w_a, w_b, w_z, *, tm=1024, tn=1024):
    T = x2d.shape[0]
    n_a = w_a.shape[1] // tn
    grid = (T // tm, P_COLS // tn)
    return pl.pallas_call(
        functools.partial(_inproj_kernel, rows=128, n_a=n_a),
        grid=grid,
        in_specs=[
            pl.BlockSpec((tm, D_MODEL), lambda i, j: (i, 0)),
            pl.BlockSpec((1, D_MODEL), lambda i, j: (0, 0)),
            pl.BlockSpec((D_MODEL, tn), lambda i, j: (0, jnp.minimum(j, n_a - 1))),
            pl.BlockSpec((D_MODEL, tn), lambda i, j: (0, jnp.maximum(j - n_a, 0))),
            pl.BlockSpec((D_MODEL, Z_COLS), lambda i, j: (0, 0)),
        ],
        out_specs=[
            pl.BlockSpec((tm, tn), lambda i, j: (i, j)),
            pl.BlockSpec((tm, Z_COLS), lambda i, j: (i, 0)),
        ],
        out_shape=[
            jax.ShapeDtypeStruct((T, P_COLS), BF16),
            jax.ShapeDtypeStruct((T, Z_COLS), F32),
        ],
        scratch_shapes=[pltpu.VMEM((tm, D_MODEL), BF16)],
        compiler_params=_cparams(("parallel", "arbitrary"), 52),
        name="inproj",
    )(x2d, g_pre, w_a, w_b, w_z)


def _gla_kernel(qf_ref, kf_ref, vf_ref, zf_ref, qb_ref, kb_ref, vb_ref, zb_ref,
                waf_ref, baf_ref, wab_ref, bab_ref, of_ref, ob_ref,
                st_ref, la_ref, b_ref, qk_ref, x_ref, d_ref, p_ref, oi_ref):
    @pl.when(pl.program_id(1) == 0)
    def _():
        st_ref[...] = jnp.zeros_like(st_ref)

    def log_decay(z_ref, w_ref, bias_ref):
        pre = jnp.dot(z_ref[...].astype(BF16), w_ref[...], preferred_element_type=F32) + bias_ref[...]
        return _log_sigmoid(pre) / GATE_TAU

    la_ref[0] = log_decay(zf_ref, waf_ref, baf_ref)
    la_ref[1] = log_decay(zb_ref, wab_ref, bab_ref)

    C = CHUNK
    row3 = lax.broadcasted_iota(jnp.int32, (C, 3 * C), 0)
    col3 = lax.broadcasted_iota(jnp.int32, (C, 3 * C), 1) & (C - 1)
    row = lax.broadcasted_iota(jnp.int32, (C, C), 0)
    col = lax.broadcasted_iota(jnp.int32, (C, C), 1)
    dirs = (
        (qf_ref, kf_ref, vf_ref, of_ref, (col3 <= row3).astype(BF16), col <= row, C // 2, C - 1, 0),
        (qb_ref, kb_ref, vb_ref, ob_ref, (col3 >= row3).astype(BF16), col >= row, C - 1 - C // 2, 0, 1),
    )
    n_blocks = qf_ref.shape[0] // GLA_BLOCK
    chains = [(dr, hd) for dr in range(2) for hd in range(N_HEADS)]
    klanes = lambda hd: slice(hd * HEAD_DK, (hd + 1) * HEAD_DK)
    vlanes = lambda hd: slice(hd * HEAD_DV, (hd + 1) * HEAD_DV)
    sub = lambda sc: slice(sc * C, (sc + 1) * C)
    zeros_cc = jnp.zeros((C, C), F32)

    def body(c, carry):
        rows = (pl.ds(pl.multiple_of(c * GLA_BLOCK, GLA_BLOCK), GLA_BLOCK),
                pl.ds(pl.multiple_of((n_blocks - 1 - c) * GLA_BLOCK, GLA_BLOCK), GLA_BLOCK))

        for dr in range(2):
            tri3, last_idx, first = dirs[dr][4], dirs[dr][7], dirs[dr][8]
            la = la_ref[dr, rows[dr], :]
            h1 = la.astype(BF16)
            r1 = la - h1.astype(F32)
            h2 = r1.astype(BF16)
            h3 = (r1 - h2.astype(F32)).astype(BF16)
            total = []
            for sc in range(2):
                b = jnp.dot(tri3, jnp.concatenate([h1[sub(sc)], h2[sub(sc)], h3[sub(sc)]], axis=0),
                            preferred_element_type=F32)
                b_ref[dr, sub(sc), :] = b
                total.append(b[last_idx:last_idx + 1, :])
            d_ref[dr, 0:1, :] = jnp.exp(total[0] + total[1])
            d_ref[dr, 1:2, :] = jnp.exp(total[first])
            d_ref[dr, 2:3, :] = jnp.exp(total[1 - first])

        for dr, hd in chains:
            q_ref, k_ref, ref_idx, last_idx, first = (dirs[dr][0], dirs[dr][1], dirs[dr][6],
                                                      dirs[dr][7], dirs[dr][8])
            kl = klanes(hd)
            b = b_ref[dr, :, kl]

            def per_chunk(idx):
                return jnp.concatenate([jnp.broadcast_to(b[sc * C + idx:sc * C + idx + 1, :], (C, HEAD_DK))
                                        for sc in range(2)], axis=0)

            b_mid, b_last = per_chunk(ref_idx), per_chunk(last_idx)
            q = q_ref[rows[dr], kl].astype(F32) * (HEAD_DK ** -0.5)
            k = k_ref[rows[dr], kl].astype(F32)
            q_loc = q * jnp.exp(b)
            k_loc = k * jnp.exp(b_last - b)
            qk_ref[dr, 1, :, kl] = (q * jnp.exp(b - b_mid)).astype(BF16)
            qk_ref[dr, 2, :, kl] = (k * jnp.exp(b_mid - b)).astype(BF16)
            early, late = sub(first), sub(1 - first)
            qk_ref[dr, 0, early, kl] = q_loc[early].astype(BF16)
            qk_ref[dr, 0, late, kl] = (q_loc[late] * d_ref[dr, 1:2, kl]).astype(BF16)
            qk_ref[dr, 3, early, kl] = (k_loc[early] * d_ref[dr, 2:3, kl]).astype(BF16)
            qk_ref[dr, 3, late, kl] = k_loc[late].astype(BF16)
            x_ref[dr, 0, :, kl] = q_loc[late].astype(BF16)
            x_ref[dr, 1, :, kl] = k_loc[early].astype(BF16)

        for dr, hd in chains:
            mask, first = dirs[dr][5], dirs[dr][8]
            kl = klanes(hd)
            diag = []
            for sc in range(2):
                s = lax.dot_general(qk_ref[dr, 1, sub(sc), kl], qk_ref[dr, 2, sub(sc), kl], _NT,
                                    preferred_element_type=F32)
                diag.append(jnp.where(mask, s, 0.0))
            cross = lax.dot_general(x_ref[dr, 0, :, kl], x_ref[dr, 1, :, kl], _NT, preferred_element_type=F32)
            if first == 0:
                top = jnp.concatenate([diag[0], zeros_cc], axis=1)
                bottom = jnp.concatenate([cross, diag[1]], axis=1)
            else:
                top = jnp.concatenate([diag[0], cross], axis=1)
                bottom = jnp.concatenate([zeros_cc, diag[1]], axis=1)
            p_ref[dr * N_HEADS + hd] = jnp.concatenate([top, bottom], axis=0).astype(BF16)

        for dr, hd in chains:
            v_ref = dirs[dr][2]
            ch = dr * N_HEADS + hd
            st = st_ref[ch]
            oi_ref[ch] = lax.dot_general(qk_ref[dr, 0, :, klanes(hd)], st.astype(BF16), _NT,
                                         preferred_element_type=F32)
            upd = lax.dot_general(v_ref[rows[dr], vlanes(hd)], qk_ref[dr, 3, :, klanes(hd)], _TN,
                                  preferred_element_type=F32)
            st_ref[ch] = d_ref[dr, 0:1, klanes(hd)] * st + upd

        for dr, hd in chains:
            v_ref, o_ref = dirs[dr][2], dirs[dr][3]
            ch = dr * N_HEADS + hd
            o = oi_ref[ch] + jnp.dot(p_ref[ch], v_ref[rows[dr], vlanes(hd)], preferred_element_type=F32)
            o_ref[rows[dr], vlanes(hd)] = o.astype(o_ref.dtype)
        return carry

    lax.fori_loop(0, n_blocks, body, 0)


def _gla(P, Z, wa_f, ba_f, wa_b, ba_b, B, S, *, ts=512):
    T = B * S
    nb = S // ts
    fwd = lambda b, i: b * nb + i
    bwd = lambda b, i: b * nb + (nb - 1 - i)

    def stream_specs(rowmap):
        return [
            pl.BlockSpec((ts, D_K), lambda b, i: (rowmap(b, i), 0)),
            pl.BlockSpec((ts, D_K), lambda b, i: (rowmap(b, i), 1)),
            pl.BlockSpec((ts, D_V), lambda b, i: (rowmap(b, i), 1)),
            pl.BlockSpec((ts, Z_COLS), lambda b, i: (rowmap(b, i), 0)),
        ]

    w_spec = pl.BlockSpec((Z_COLS, D_K), lambda b, i: (0, 0))
    b_spec = pl.BlockSpec((1, D_K), lambda b, i: (0, 0))
    return pl.pallas_call(
        _gla_kernel,
        grid=(B, nb),
        in_specs=stream_specs(fwd) + stream_specs(bwd) + [w_spec, b_spec, w_spec, b_spec],
        out_specs=[
            pl.BlockSpec((ts, D_V), lambda b, i: (fwd(b, i), 0)),
            pl.BlockSpec((ts, D_V), lambda b, i: (bwd(b, i), 0)),
        ],
        out_shape=[jax.ShapeDtypeStruct((T, D_V), BF16)] * 2,
        scratch_shapes=[
            pltpu.VMEM((2 * N_HEADS, HEAD_DV, HEAD_DK), F32),
            pltpu.VMEM((2, ts, D_K), F32),
            pltpu.VMEM((2, GLA_BLOCK, D_K), F32),
            pltpu.VMEM((2, 4, GLA_BLOCK, D_K), BF16),
            pltpu.VMEM((2, 2, CHUNK, D_K), BF16),
            pltpu.VMEM((2, 8, D_K), F32),
            pltpu.VMEM((2 * N_HEADS, GLA_BLOCK, GLA_BLOCK), BF16),
            pltpu.VMEM((2 * N_HEADS, GLA_BLOCK, HEAD_DV), F32),
        ],
        compiler_params=_cparams(("parallel", "arbitrary"), 56),
        name="gla",
    )(P, P, P, Z, P, P, P, Z, wa_f, ba_f, wa_b, ba_b)


def _conv_kernel(pa_ref, pb_ref, pa_prev, pb_prev, pa_next, pb_next,
                 cw_ref, cb_ref, lg_ref, lb_ref, w2_ref, out_ref,
                 glu_ref, dw_ref, act_ref, *, rb, lb, nt):
    s = pl.program_id(0)
    tile = jnp.minimum(s, pl.num_programs(0) - 2)
    i = tile % nt
    slot = s % 2
    tt = pa_ref.shape[0]

    @pl.when(s == 0)
    def _():
        act_ref[1] = jnp.zeros(act_ref.shape[1:], act_ref.dtype)

    def glu(a, b):
        return a.astype(F32) * _sigmoid(b.astype(F32))

    n_win = tt // rb
    head = rb + HALO
    glu_ref[0, 0:HALO, :] = jnp.where(i > 0, glu(pa_prev[...], pb_prev[...]), 0.0)
    glu_ref[0, HALO:HALO + head, :] = glu(pa_ref[0:head, :], pb_ref[0:head, :])
    glu_ref[n_win - 1, 0:head, :] = glu(pa_ref[tt - head:tt, :], pb_ref[tt - head:tt, :])
    glu_ref[n_win - 1, head:head + HALO, :] = jnp.where(i < nt - 1, glu(pa_next[...], pb_next[...]), 0.0)

    def fill_body(m, c):
        src = pl.ds(pl.multiple_of(m * rb - HALO, HALO), rb + 2 * HALO)
        glu_ref[m] = glu(pa_ref[src, :], pb_ref[src, :])
        return c

    lax.fori_loop(1, n_win - 1, fill_body, 0)

    sub = 8
    shift = HALO - CONV_PAD
    n_col = D_MODEL // n_win

    for m in range(n_win):
        cols = slice(m * n_col, (m + 1) * n_col)
        out_ref[:, cols] = jnp.dot(act_ref[1 - slot], w2_ref[:, cols],
                                   preferred_element_type=F32).astype(out_ref.dtype)
        out_rows = slice(m * rb, (m + 1) * rb)
        for l0 in range(0, D_MODEL, lb):
            lanes = slice(l0, l0 + lb)
            acc = None
            for ph in range(sub):
                part = None
                for q in range((CONV_WIDTH + shift) // sub + 1):
                    j = sub * q + ph - shift
                    if 0 <= j < CONV_WIDTH:
                        term = cw_ref[j:j + 1, lanes] * glu_ref[m, pl.ds(sub * q, rb + sub), lanes]
                        part = term if part is None else part + term
                shifted = part[ph:ph + rb]
                acc = shifted if acc is None else acc + shifted
            dw_ref[out_rows, lanes] = acc + cb_ref[:, lanes]

    ln_rows = 64

    def ln_body(r, c):
        sl = pl.ds(pl.multiple_of(r * ln_rows, ln_rows), ln_rows)
        x = dw_ref[sl, :]
        mu = jnp.mean(x, axis=-1, keepdims=True)
        xc = x - mu
        var = jnp.mean(xc * xc, axis=-1, keepdims=True)
        y = xc * lax.rsqrt(var + EPS) * lg_ref[...] + lb_ref[...]
        act_ref[slot, sl, :] = (y * _sigmoid(y)).astype(BF16)
        return c

    lax.fori_loop(0, tt // ln_rows, ln_body, 0)


def _conv_module(P, conv_w, conv_b, ln_g, ln_b, w_pw2, B, S, *, tt=512):
    T = B * S
    nt = S // tt
    n_tiles = B * nt
    hb = tt // HALO
    n_halo_blocks = T // HALO
    ca, cb = OFF_PWA // D_MODEL, OFF_PWB // D_MODEL
    tile = lambda s: jnp.minimum(s, n_tiles - 1)
    prev = lambda s: jnp.maximum(tile(s) * hb - 1, 0)
    nxt = lambda s: jnp.minimum((tile(s) + 1) * hb, n_halo_blocks - 1)
    vec = lambda n: pl.BlockSpec((n, D_MODEL), lambda s: (0, 0))
    rb = 64
    return pl.pallas_call(
        functools.partial(_conv_kernel, rb=rb, lb=128, nt=nt),
        grid=(n_tiles + 1,),
        in_specs=[
            pl.BlockSpec((tt, D_MODEL), lambda s: (tile(s), ca)),
            pl.BlockSpec((tt, D_MODEL), lambda s: (tile(s), cb)),
            pl.BlockSpec((HALO, D_MODEL), lambda s: (prev(s), ca)),
            pl.BlockSpec((HALO, D_MODEL), lambda s: (prev(s), cb)),
            pl.BlockSpec((HALO, D_MODEL), lambda s: (nxt(s), ca)),
            pl.BlockSpec((HALO, D_MODEL), lambda s: (nxt(s), cb)),
            vec(CONV_WIDTH), vec(1), vec(1), vec(1),
            pl.BlockSpec((D_MODEL, D_MODEL), lambda s: (0, 0)),
        ],
        out_specs=pl.BlockSpec((tt, D_MODEL), lambda s: (jnp.maximum(s - 1, 0), 0)),
        out_shape=jax.ShapeDtypeStruct((T, D_MODEL), BF16),
        scratch_shapes=[
            pltpu.VMEM((tt // rb, rb + 2 * HALO, D_MODEL), F32),
            pltpu.VMEM((tt, D_MODEL), F32),
            pltpu.VMEM((2, tt, D_MODEL), BF16),
        ],
        compiler_params=_cparams(("arbitrary",), 52),
        name="conv_module",
    )(P, P, P, P, P, P, conv_w, conv_b, ln_g, ln_b, w_pw2)


def _merge_kernel(of_ref, ob_ref, g_ref, ga_ref, gb_ref, cb_ref, x_ref, gn_ref, gpost_ref, wo_ref,
                  h_ref, m_ref, *, rows, norm_rows):
    s = pl.program_id(0)
    slot = s % 2
    tt = x_ref.shape[0]
    n_blk = tt // rows
    n_col = D_MODEL // n_blk

    @pl.when(s == 0)
    def _():
        m_ref[1] = jnp.zeros(m_ref.shape[1:], m_ref.dtype)

    for r in range(n_blk):
        cols = slice(r * n_col, (r + 1) * n_col)
        h_ref[:, cols] = jnp.dot(m_ref[1 - slot], wo_ref[:, cols], preferred_element_type=F32)
        sl = slice(r * rows, (r + 1) * rows)
        o = of_ref[sl, :].astype(F32) + ob_ref[sl, :].astype(F32)
        heads = []
        for hd in range(N_HEADS):
            oh = o[:, hd * HEAD_DV:(hd + 1) * HEAD_DV]
            ms = jnp.mean(oh * oh, axis=-1, keepdims=True)
            heads.append(oh * lax.rsqrt(ms + EPS))
        on = jnp.concatenate(heads, axis=-1) * gn_ref[...]
        g = g_ref[sl, :].astype(F32)
        out_a = on * (g * _sigmoid(g))
        merged = (_sigmoid(ga_ref[sl, :].astype(F32)) * out_a
                  + _sigmoid(gb_ref[sl, :].astype(F32)) * cb_ref[sl, :].astype(F32))
        m_ref[slot, sl, :] = merged.astype(BF16)

    def norm_body(r, c):
        sl = pl.ds(pl.multiple_of(r * norm_rows, norm_rows), norm_rows)
        y = h_ref[sl, :]
        ms = jnp.mean(y * y, axis=-1, keepdims=True)
        h_ref[sl, :] = x_ref[sl, :] + (y * lax.rsqrt(ms + EPS)) * gpost_ref[...]
        return c

    lax.fori_loop(0, tt // norm_rows, norm_body, 0)


def _merge(o_f, o_b, P, out_b, x2d, gla_norm, g_post, w_out, *, tt=256):
    T = x2d.shape[0]
    n_tiles = T // tt
    cur = lambda c: pl.BlockSpec((tt, D_MODEL), lambda s: (jnp.minimum(s, n_tiles - 1), c))
    lag = pl.BlockSpec((tt, D_MODEL), lambda s: (jnp.maximum(s - 1, 0), 0))
    vec = pl.BlockSpec((1, D_MODEL), lambda s: (0, 0))
    return pl.pallas_call(
        functools.partial(_merge_kernel, rows=32, norm_rows=64),
        grid=(n_tiles + 1,),
        in_specs=[
            cur(0), cur(0),
            cur(OFF_G // D_MODEL), cur(OFF_GA // D_MODEL), cur(OFF_GB // D_MODEL),
            cur(0), lag, vec, vec,
            pl.BlockSpec((D_MODEL, D_MODEL), lambda s: (0, 0)),
        ],
        out_specs=lag,
        out_shape=jax.ShapeDtypeStruct((T, D_MODEL), F32),
        scratch_shapes=[pltpu.VMEM((2, tt, D_MODEL), BF16)],
        compiler_params=_cparams(("arbitrary",), 48),
        name="merge_outproj",
    )(o_f, o_b, P, P, P, out_b, x2d, gla_norm, g_post, w_out)


def _ffn_kernel(h_ref, gpre_ref, gpost_ref, w1_ref, w2_ref, y_ref, u_ref, acc_ref, *, rows):
    f = pl.program_id(1)
    tm = h_ref.shape[0]

    @pl.when(f == 0)
    def _():
        def body(r, c):
            sl = pl.ds(pl.multiple_of(r * rows, rows), rows)
            x = h_ref[sl, :]
            ms = jnp.mean(x * x, axis=-1, keepdims=True)
            u_ref[sl, :] = ((x * lax.rsqrt(ms + EPS)) * gpre_ref[...]).astype(BF16)
            return c
        lax.fori_loop(0, tm // rows, body, 0)
        acc_ref[...] = jnp.zeros_like(acc_ref)

    a = jnp.dot(u_ref[...], w1_ref[...], preferred_element_type=F32)
    a = jnp.maximum(a, 0.0)
    a = (a * a).astype(BF16)
    acc_ref[...] += jnp.dot(a, w2_ref[...], preferred_element_type=F32)

    @pl.when(f == pl.num_programs(1) - 1)
    def _():
        def body(r, c):
            sl = pl.ds(pl.multiple_of(r * rows, rows), rows)
            y = acc_ref[sl, :]
            ms = jnp.mean(y * y, axis=-1, keepdims=True)
            y_ref[sl, :] = h_ref[sl, :] + (y * lax.rsqrt(ms + EPS)) * gpost_ref[...]
            return c
        lax.fori_loop(0, tm // rows, body, 0)


def _ffn(h2d, g_pre, g_post, w1, w2, *, tm=512, tf=1024):
    T = h2d.shape[0]
    vec = pl.BlockSpec((1, D_MODEL), lambda i, f: (0, 0))
    return pl.pallas_call(
        functools.partial(_ffn_kernel, rows=128),
        grid=(T // tm, D_FF // tf),
        in_specs=[
            pl.BlockSpec((tm, D_MODEL), lambda i, f: (i, 0)),
            vec, vec,
            pl.BlockSpec((D_MODEL, tf), lambda i, f: (0, f)),
            pl.BlockSpec((tf, D_MODEL), lambda i, f: (f, 0)),
        ],
        out_specs=pl.BlockSpec((tm, D_MODEL), lambda i, f: (i, 0)),
        out_shape=jax.ShapeDtypeStruct((T, D_MODEL), F32),
        scratch_shapes=[pltpu.VMEM((tm, D_MODEL), BF16), pltpu.VMEM((tm, D_MODEL), F32)],
        compiler_params=_cparams(("parallel", "arbitrary"), 48),
        name="ffn",
    )(h2d, g_pre, g_post, w1, w2)


def _encoder_layer(x, wts):
    B, S, _ = x.shape
    x2d = x.reshape(B * S, D_MODEL)
    P, Z = _inproj(x2d, wts["g_mix_pre"], wts["w_a"], wts["w_b"], wts["w_z"])
    o_f, o_b = _gla(P, Z, wts["wa_f"], wts["ba_f"], wts["wa_b"], wts["ba_b"], B, S)
    out_b = _conv_module(P, wts["conv_w"], wts["conv_b"], wts["ln_g"], wts["ln_b"], wts["w_pw2"], B, S)
    h = _merge(o_f, o_b, P, out_b, x2d, wts["gla_norm"], wts["g_mix_post"], wts["w_out"])
    y = _ffn(h, wts["g_ffn_pre"], wts["g_ffn_post"], wts["w_ff1"], wts["w_ff2"])
    return y.reshape(B, S, D_MODEL)


def kernel(x_prompt, x_sample, norm_mix_pre, norm_mix_post, norm_ffn_pre, norm_ffn_post, w_in, w_a_fwd, b_a_fwd, w_a_bwd, b_a_bwd, gla_norm, conv_w, conv_b, conv_ln_g, conv_ln_b, w_pw2, w_out, w_ff1, w_ff2):
    z0 = 2 * D_K + 2 * D_V
    z1 = z0 + 2 * GATE_RANK
    row = lambda v: v.reshape(1, -1).astype(F32)
    w_z = jnp.zeros((D_MODEL, Z_COLS), F32).at[:, :2 * GATE_RANK].set(w_in[:, z0:z1])
    wa_f = jnp.zeros((Z_COLS, D_K), F32).at[:GATE_RANK].set(w_a_fwd)
    wa_b = jnp.zeros((Z_COLS, D_K), F32).at[GATE_RANK:2 * GATE_RANK].set(w_a_bwd)
    wts = {
        "g_mix_pre": row(norm_mix_pre), "g_mix_post": row(norm_mix_post),
        "g_ffn_pre": row(norm_ffn_pre), "g_ffn_post": row(norm_ffn_post),
        "w_a": w_in[:, :z0].astype(BF16), "w_b": w_in[:, z1:].astype(BF16),
        "w_z": w_z.astype(BF16),
        "wa_f": wa_f.astype(BF16), "ba_f": row(b_a_fwd),
        "wa_b": wa_b.astype(BF16), "ba_b": row(b_a_bwd),
        "gla_norm": row(gla_norm),
        "conv_w": conv_w, "conv_b": row(conv_b),
        "ln_g": row(conv_ln_g), "ln_b": row(conv_ln_b),
        "w_pw2": w_pw2.astype(BF16), "w_out": w_out.astype(BF16),
        "w_ff1": w_ff1.astype(BF16), "w_ff2": w_ff2.astype(BF16),
    }
    return (_encoder_layer(x_prompt, wts), _encoder_layer(x_sample, wts))
```

```python
import functools

import jax
import jax.numpy as jnp
from jax import lax
from jax.experimental import pallas as pl
from jax.experimental.pallas import tpu as pltpu

F32 = jnp.float32
BF16 = jnp.bfloat16

D_MODEL = 2048
N_HEADS = 4
D_K = D_MODEL // 2
D_V = D_MODEL
HEAD_DK = D_K // N_HEADS
HEAD_DV = D_V // N_HEADS
GATE_RANK = 16
GATE_TAU = 16.0
CHUNK = 64
CONV_WIDTH = 31
CONV_PAD = (CONV_WIDTH - 1) // 2
D_FF = 4 * D_MODEL
EPS = 1e-6

P_COLS = 7 * D_MODEL
OFF_G = 2 * D_MODEL
OFF_PWA = 3 * D_MODEL
OFF_PWB = 4 * D_MODEL
OFF_GA = 5 * D_MODEL
OFF_GB = 6 * D_MODEL
Z_COLS = 128

HALO = 16
GLA_BLOCK = 2 * CHUNK

_NT = (((1,), (1,)), ((), ()))
_TN = (((0,), (0,)), ((), ()))

_MIB = 1024 * 1024


def _cparams(sem, vmem_mib):
    return pltpu.CompilerParams(dimension_semantics=sem, vmem_limit_bytes=vmem_mib * _MIB)


def _sigmoid(x):
    return 1.0 / (1.0 + jnp.exp(-x))


def _log_sigmoid(x):
    return jnp.minimum(x, 0.0) - jnp.log(1.0 + jnp.exp(-jnp.abs(x)))


def _inproj_kernel(x_ref, g_ref, w_ref, wz_ref, p_ref, z_ref, u_ref, *, rows):
    @pl.when(pl.program_id(1) == 0)
    def _():
        def body(r, c):
            sl = pl.ds(pl.multiple_of(r * rows, rows), rows)
            x = x_ref[sl, :]
            ms = jnp.mean(x * x, axis=-1, keepdims=True)
            u = (x * lax.rsqrt(ms + EPS)) * g_ref[...]
            u_ref[sl, :] = u.astype(BF16)
            return c
        lax.fori_loop(0, x_ref.shape[0] // rows, body, 0)
        z_ref[...] = jnp.dot(u_ref[...], wz_ref[...], preferred_element_type=F32)

    p_ref[...] = jnp.dot(u_ref[...], w_ref[...], preferred_element_type=F32).astype(p_ref.dtype)


def _inproj(x2d, g_pre, w_main, w_z, *, tm=1024, tn=2048):
    T = x2d.shape[0]
    grid = (T // tm, P_COLS // tn)
    return pl.pallas_call(
        functools.partial(_inproj_kernel, rows=128),
        grid=grid,
        in_specs=[
            pl.BlockSpec((tm, D_MODEL), lambda i, j: (i, 0)),
            pl.BlockSpec((1, D_MODEL), lambda i, j: (0, 0)),
            pl.BlockSpec((D_MODEL, tn), lambda i, j: (0, j)),
            pl.BlockSpec((D_MODEL, Z_COLS), lambda i, j: (0, 0)),
        ],
        out_specs=[
            pl.BlockSpec((tm, tn), lambda i, j: (i, j)),
            pl.BlockSpec((tm, Z_COLS), lambda i, j: (i, 0)),
        ],
        out_shape=[
            jax.ShapeDtypeStruct((T, P_COLS), BF16),
            jax.ShapeDtypeStruct((T, Z_COLS), F32),
        ],
        scratch_shapes=[pltpu.VMEM((tm, D_MODEL), BF16)],
        compiler_params=_cparams(("parallel", "arbitrary"), 52),
        name="inproj",
    )(x2d, g_pre, w_main, w_z)


def _gla_kernel(qf_ref, kf_ref, vf_ref, zf_ref, qb_ref, kb_ref, vb_ref, zb_ref,
                waf_ref, baf_ref, wab_ref, bab_ref, of_ref, ob_ref,
                st_ref, la_ref, b_ref, qk_ref, x_ref, d_ref, p_ref, oi_ref):
    @pl.when(pl.program_id(1) == 0)
    def _():
        st_ref[...] = jnp.zeros_like(st_ref)

    def log_decay(z_ref, w_ref, bias_ref):
        pre = jnp.dot(z_ref[...].astype(BF16), w_ref[...], preferred_element_type=F32) + bias_ref[...]
        return _log_sigmoid(pre) / GATE_TAU

    la_ref[0] = log_decay(zf_ref, waf_ref, baf_ref)
    la_ref[1] = log_decay(zb_ref, wab_ref, bab_ref)

    C = CHUNK
    row3 = lax.broadcasted_iota(jnp.int32, (C, 3 * C), 0)
    col3 = lax.broadcasted_iota(jnp.int32, (C, 3 * C), 1) & (C - 1)
    row = lax.broadcasted_iota(jnp.int32, (C, C), 0)
    col = lax.broadcasted_iota(jnp.int32, (C, C), 1)
    dirs = (
        (qf_ref, kf_ref, vf_ref, of_ref, (col3 <= row3).astype(BF16), col <= row, C // 2, C - 1, 0),
        (qb_ref, kb_ref, vb_ref, ob_ref, (col3 >= row3).astype(BF16), col >= row, C - 1 - C // 2, 0, 1),
    )
    n_blocks = qf_ref.shape[0] // GLA_BLOCK
    chains = [(dr, hd) for dr in range(2) for hd in range(N_HEADS)]
    klanes = lambda hd: slice(hd * HEAD_DK, (hd + 1) * HEAD_DK)
    vlanes = lambda hd: slice(hd * HEAD_DV, (hd + 1) * HEAD_DV)
    sub = lambda sc: slice(sc * C, (sc + 1) * C)
    zeros_cc = jnp.zeros((C, C), F32)

    def body(c, carry):
        rows = (pl.ds(pl.multiple_of(c * GLA_BLOCK, GLA_BLOCK), GLA_BLOCK),
                pl.ds(pl.multiple_of((n_blocks - 1 - c) * GLA_BLOCK, GLA_BLOCK), GLA_BLOCK))

        for dr in range(2):
            tri3, last_idx, first = dirs[dr][4], dirs[dr][7], dirs[dr][8]
            la = la_ref[dr, rows[dr], :]
            h1 = la.astype(BF16)
            r1 = la - h1.astype(F32)
            h2 = r1.astype(BF16)
            h3 = (r1 - h2.astype(F32)).astype(BF16)
            total = []
            for sc in range(2):
                b = jnp.dot(tri3, jnp.concatenate([h1[sub(sc)], h2[sub(sc)], h3[sub(sc)]], axis=0),
                            preferred_element_type=F32)
                b_ref[dr, sub(sc), :] = b
                total.append(b[last_idx:last_idx + 1, :])
            d_ref[dr, 0:1, :] = jnp.exp(total[0] + total[1])
            d_ref[dr, 1:2, :] = jnp.exp(total[first])
            d_ref[dr, 2:3, :] = jnp.exp(total[1 - first])

        for dr, hd in chains:
            q_ref, k_ref, ref_idx, last_idx, first = (dirs[dr][0], dirs[dr][1], dirs[dr][6],
                                                      dirs[dr][7], dirs[dr][8])
            kl = klanes(hd)
            b = b_ref[dr, :, kl]

            def per_chunk(idx):
                return jnp.concatenate([jnp.broadcast_to(b[sc * C + idx:sc * C + idx + 1, :], (C, HEAD_DK))
                                        for sc in range(2)], axis=0)

            b_mid, b_last = per_chunk(ref_idx), per_chunk(last_idx)
            q = q_ref[rows[dr], kl].astype(F32) * (HEAD_DK ** -0.5)
            k = k_ref[rows[dr], kl].astype(F32)
            q_loc = q * jnp.exp(b)
            k_loc = k * jnp.exp(b_last - b)
            qk_ref[dr, 1, :, kl] = (q * jnp.exp(b - b_mid)).astype(BF16)
            qk_ref[dr, 2, :, kl] = (k * jnp.exp(b_mid - b)).astype(BF16)
            early, late = sub(first), sub(1 - first)
            qk_ref[dr, 0, early, kl] = q_loc[early].astype(BF16)
            qk_ref[dr, 0, late, kl] = (q_loc[late] * d_ref[dr, 1:2, kl]).astype(BF16)
            qk_ref[dr, 3, early, kl] = (k_loc[early] * d_ref[dr, 2:3, kl]).astype(BF16)
            qk_ref[dr, 3, late, kl] = k_loc[late].astype(BF16)
            x_ref[dr, 0, :, kl] = q_loc[late].astype(BF16)
            x_ref[dr, 1, :, kl] = k_loc[early].astype(BF16)

        for dr, hd in chains:
            mask, first = dirs[dr][5], dirs[dr][8]
            kl = klanes(hd)
            diag = []
            for sc in range(2):
                s = lax.dot_general(qk_ref[dr, 1, sub(sc), kl], qk_ref[dr, 2, sub(sc), kl], _NT,
                                    preferred_element_type=F32)
                diag.append(jnp.where(mask, s, 0.0))
            cross = lax.dot_general(x_ref[dr, 0, :, kl], x_ref[dr, 1, :, kl], _NT, preferred_element_type=F32)
            if first == 0:
                top = jnp.concatenate([diag[0], zeros_cc], axis=1)
                bottom = jnp.concatenate([cross, diag[1]], axis=1)
            else:
                top = jnp.concatenate([diag[0], cross], axis=1)
                bottom = jnp.concatenate([zeros_cc, diag[1]], axis=1)
            p_ref[dr * N_HEADS + hd] = jnp.concatenate([top, bottom], axis=0).astype(BF16)

        for dr, hd in chains:
            v_ref = dirs[dr][2]
            ch = dr * N_HEADS + hd
            st = st_ref[ch]
            oi_ref[ch] = lax.dot_general(qk_ref[dr, 0, :, klanes(hd)], st.astype(BF16), _NT,
                                         preferred_element_type=F32)
            upd = lax.dot_general(v_ref[rows[dr], vlanes(hd)], qk_ref[dr, 3, :, klanes(hd)], _TN,
                                  preferred_element_type=F32)
            st_ref[ch] = d_ref[dr, 0:1, klanes(hd)] * st + upd

        for dr, hd in chains:
            v_ref, o_ref = dirs[dr][2], dirs[dr][3]
            ch = dr * N_HEADS + hd
            o = oi_ref[ch] + jnp.dot(p_ref[ch], v_ref[rows[dr], vlanes(hd)], preferred_element_type=F32)
            o_ref[rows[dr], vlanes(hd)] = o.astype(o_ref.dtype)
        return carry

    lax.fori_loop(0, n_blocks, body, 0)


def _gla(P, Z, wa_f, ba_f, wa_b, ba_b, B, S, *, ts=512):
    T = B * S
    nb = S // ts
    fwd = lambda b, i: b * nb + i
    bwd = lambda b, i: b * nb + (nb - 1 - i)

    def stream_specs(rowmap):
        return [
            pl.BlockSpec((ts, D_K), lambda b, i: (rowmap(b, i), 0)),
            pl.BlockSpec((ts, D_K), lambda b, i: (rowmap(b, i), 1)),
            pl.BlockSpec((ts, D_V), lambda b, i: (rowmap(b, i), 1)),
            pl.BlockSpec((ts, Z_COLS), lambda b, i: (rowmap(b, i), 0)),
        ]

    w_spec = pl.BlockSpec((Z_COLS, D_K), lambda b, i: (0, 0))
    b_spec = pl.BlockSpec((1, D_K), lambda b, i: (0, 0))
    return pl.pallas_call(
        _gla_kernel,
        grid=(B, nb),
        in_specs=stream_specs(fwd) + stream_specs(bwd) + [w_spec, b_spec, w_spec, b_spec],
        out_specs=[
            pl.BlockSpec((ts, D_V), lambda b, i: (fwd(b, i), 0)),
            pl.BlockSpec((ts, D_V), lambda b, i: (bwd(b, i), 0)),
        ],
        out_shape=[jax.ShapeDtypeStruct((T, D_V), BF16)] * 2,
        scratch_shapes=[
            pltpu.VMEM((2 * N_HEADS, HEAD_DV, HEAD_DK), F32),
            pltpu.VMEM((2, ts, D_K), F32),
            pltpu.VMEM((2, GLA_BLOCK, D_K), F32),
            pltpu.VMEM((2, 4, GLA_BLOCK, D_K), BF16),
            pltpu.VMEM((2, 2, CHUNK, D_K), BF16),
            pltpu.VMEM((2, 8, D_K), F32),
            pltpu.VMEM((2 * N_HEADS, GLA_BLOCK, GLA_BLOCK), BF16),
            pltpu.VMEM((2 * N_HEADS, GLA_BLOCK, HEAD_DV), F32),
        ],
        compiler_params=_cparams(("parallel", "arbitrary"), 56),
        name="gla",
    )(P, P, P, Z, P, P, P, Z, wa_f, ba_f, wa_b, ba_b)


def _conv_kernel(pa_ref, pb_ref, pa_prev, pb_prev, pa_next, pb_next,
                 cw_ref, cb_ref, lg_ref, lb_ref, w2_ref, out_ref,
                 glu_ref, dw_ref, act_ref, *, rb, lb, nt):
    s = pl.program_id(0)
    tile = jnp.minimum(s, pl.num_programs(0) - 2)
    i = tile % nt
    slot = s % 2
    tt = pa_ref.shape[0]

    @pl.when(s == 0)
    def _():
        act_ref[1] = jnp.zeros(act_ref.shape[1:], act_ref.dtype)

    def glu(a, b):
        return a.astype(F32) * _sigmoid(b.astype(F32))

    n_win = tt // rb
    head = rb + HALO
    glu_ref[0, 0:HALO, :] = jnp.where(i > 0, glu(pa_prev[...], pb_prev[...]), 0.0)
    glu_ref[0, HALO:HALO + head, :] = glu(pa_ref[0:head, :], pb_ref[0:head, :])
    glu_ref[n_win - 1, 0:head, :] = glu(pa_ref[tt - head:tt, :], pb_ref[tt - head:tt, :])
    glu_ref[n_win - 1, head:head + HALO, :] = jnp.where(i < nt - 1, glu(pa_next[...], pb_next[...]), 0.0)

    def fill_body(m, c):
        src = pl.ds(pl.multiple_of(m * rb - HALO, HALO), rb + 2 * HALO)
        glu_ref[m] = glu(pa_ref[src, :], pb_ref[src, :])
        return c

    lax.fori_loop(1, n_win - 1, fill_body, 0)

    sub = 8
    shift = HALO - CONV_PAD
    n_col = D_MODEL // n_win

    for m in range(n_win):
        cols = slice(m * n_col, (m + 1) * n_col)
        out_ref[:, cols] = jnp.dot(act_ref[1 - slot], w2_ref[:, cols],
                                   preferred_element_type=F32).astype(out_ref.dtype)
        out_rows = slice(m * rb, (m + 1) * rb)
        for l0 in range(0, D_MODEL, lb):
            lanes = slice(l0, l0 + lb)
            acc = None
            for ph in range(sub):
                part = None
                for q in range((CONV_WIDTH + shift) // sub + 1):
                    j = sub * q + ph - shift
                    if 0 <= j < CONV_WIDTH:
                        term = cw_ref[j:j + 1, lanes] * glu_ref[m, pl.ds(sub * q, rb + sub), lanes]
                        part = term if part is None else part + term
                shifted = part[ph:ph + rb]
                acc = shifted if acc is None else acc + shifted
            dw_ref[out_rows, lanes] = acc + cb_ref[:, lanes]

    ln_rows = 64

    def ln_body(r, c):
        sl = pl.ds(pl.multiple_of(r * ln_rows, ln_rows), ln_rows)
        x = dw_ref[sl, :]
        mu = jnp.mean(x, axis=-1, keepdims=True)
        xc = x - mu
        var = jnp.mean(xc * xc, axis=-1, keepdims=True)
        y = xc * lax.rsqrt(var + EPS) * lg_ref[...] + lb_ref[...]
        act_ref[slot, sl, :] = (y * _sigmoid(y)).astype(BF16)
        return c

    lax.fori_loop(0, tt // ln_rows, ln_body, 0)


def _conv_module(P, conv_w, conv_b, ln_g, ln_b, w_pw2, B, S, *, tt=512):
    T = B * S
    nt = S // tt
    n_tiles = B * nt
    hb = tt // HALO
    n_halo_blocks = T // HALO
    ca, cb = OFF_PWA // D_MODEL, OFF_PWB // D_MODEL
    tile = lambda s: jnp.minimum(s, n_tiles - 1)
    prev = lambda s: jnp.maximum(tile(s) * hb - 1, 0)
    nxt = lambda s: jnp.minimum((tile(s) + 1) * hb, n_halo_blocks - 1)
    vec = lambda n: pl.BlockSpec((n, D_MODEL), lambda s: (0, 0))
    rb = 64
    return pl.pallas_call(
        functools.partial(_conv_kernel, rb=rb, lb=128, nt=nt),
        grid=(n_tiles + 1,),
        in_specs=[
            pl.BlockSpec((tt, D_MODEL), lambda s: (tile(s), ca)),
            pl.BlockSpec((tt, D_MODEL), lambda s: (tile(s), cb)),
            pl.BlockSpec((HALO, D_MODEL), lambda s: (prev(s), ca)),
            pl.BlockSpec((HALO, D_MODEL), lambda s: (prev(s), cb)),
            pl.BlockSpec((HALO, D_MODEL), lambda s: (nxt(s), ca)),
            pl.BlockSpec((HALO, D_MODEL), lambda s: (nxt(s), cb)),
            vec(CONV_WIDTH), vec(1), vec(1), vec(1),
            pl.BlockSpec((D_MODEL, D_MODEL), lambda s: (0, 0), pipeline_mode=pl.Buffered(1)),
        ],
        out_specs=pl.BlockSpec((tt, D_MODEL), lambda s: (jnp.maximum(s - 1, 0), 0)),
        out_shape=jax.ShapeDtypeStruct((T, D_MODEL), BF16),
        scratch_shapes=[
            pltpu.VMEM((tt // rb, rb + 2 * HALO, D_MODEL), F32),
            pltpu.VMEM((tt, D_MODEL), F32),
            pltpu.VMEM((2, tt, D_MODEL), BF16),
        ],
        compiler_params=_cparams(("arbitrary",), 52),
        name="conv_module",
    )(P, P, P, P, P, P, conv_w, conv_b, ln_g, ln_b, w_pw2)


def _merge_kernel(of_ref, ob_ref, g_ref, ga_ref, gb_ref, cb_ref, x_ref, gn_ref, gpost_ref, wo_ref,
                  h_ref, m_ref, *, rows, norm_rows):
    s = pl.program_id(0)
    slot = s % 2
    tt = x_ref.shape[0]
    n_blk = tt // rows
    n_col = D_MODEL // n_blk

    @pl.when(s == 0)
    def _():
        m_ref[1] = jnp.zeros(m_ref.shape[1:], m_ref.dtype)

    for r in range(n_blk):
        cols = slice(r * n_col, (r + 1) * n_col)
        h_ref[:, cols] = jnp.dot(m_ref[1 - slot], wo_ref[:, cols], preferred_element_type=F32)
        sl = slice(r * rows, (r + 1) * rows)
        o = of_ref[sl, :].astype(F32) + ob_ref[sl, :].astype(F32)
        heads = []
        for hd in range(N_HEADS):
            oh = o[:, hd * HEAD_DV:(hd + 1) * HEAD_DV]
            ms = jnp.mean(oh * oh, axis=-1, keepdims=True)
            heads.append(oh * lax.rsqrt(ms + EPS))
        on = jnp.concatenate(heads, axis=-1) * gn_ref[...]
        g = g_ref[sl, :].astype(F32)
        out_a = on * (g * _sigmoid(g))
        merged = (_sigmoid(ga_ref[sl, :].astype(F32)) * out_a
                  + _sigmoid(gb_ref[sl, :].astype(F32)) * cb_ref[sl, :].astype(F32))
        m_ref[slot, sl, :] = merged.astype(BF16)

    def norm_body(r, c):
        sl = pl.ds(pl.multiple_of(r * norm_rows, norm_rows), norm_rows)
        y = h_ref[sl, :]
        ms = jnp.mean(y * y, axis=-1, keepdims=True)
        h_ref[sl, :] = x_ref[sl, :] + (y * lax.rsqrt(ms + EPS)) * gpost_ref[...]
        return c

    lax.fori_loop(0, tt // norm_rows, norm_body, 0)


def _merge(o_f, o_b, P, out_b, x2d, gla_norm, g_post, w_out, *, tt=512):
    T = x2d.shape[0]
    n_tiles = T // tt
    cur = lambda c: pl.BlockSpec((tt, D_MODEL), lambda s: (jnp.minimum(s, n_tiles - 1), c))
    lag = pl.BlockSpec((tt, D_MODEL), lambda s: (jnp.maximum(s - 1, 0), 0))
    vec = pl.BlockSpec((1, D_MODEL), lambda s: (0, 0))
    return pl.pallas_call(
        functools.partial(_merge_kernel, rows=64, norm_rows=64),
        grid=(n_tiles + 1,),
        in_specs=[
            cur(0), cur(0),
            cur(OFF_G // D_MODEL), cur(OFF_GA // D_MODEL), cur(OFF_GB // D_MODEL),
            cur(0), lag, vec, vec,
            pl.BlockSpec((D_MODEL, D_MODEL), lambda s: (0, 0), pipeline_mode=pl.Buffered(1)),
        ],
        out_specs=lag,
        out_shape=jax.ShapeDtypeStruct((T, D_MODEL), F32),
        scratch_shapes=[pltpu.VMEM((2, tt, D_MODEL), BF16)],
        compiler_params=_cparams(("arbitrary",), 56),
        name="merge_outproj",
    )(o_f, o_b, P, P, P, out_b, x2d, gla_norm, g_post, w_out)


def _ffn_kernel(h_ref, gpre_ref, gpost_ref, w1_ref, w2_ref, y_ref, u_ref, acc_ref, *, rows):
    f = pl.program_id(1)
    tm = h_ref.shape[0]

    @pl.when(f == 0)
    def _():
        def body(r, c):
            sl = pl.ds(pl.multiple_of(r * rows, rows), rows)
            x = h_ref[sl, :]
            ms = jnp.mean(x * x, axis=-1, keepdims=True)
            u_ref[sl, :] = ((x * lax.rsqrt(ms + EPS)) * gpre_ref[...]).astype(BF16)
            return c
        lax.fori_loop(0, tm // rows, body, 0)
        acc_ref[...] = jnp.zeros_like(acc_ref)

    a = jnp.dot(u_ref[...], w1_ref[...], preferred_element_type=F32)
    a = jnp.maximum(a, 0.0)
    a = (a * a).astype(BF16)
    acc_ref[...] += jnp.dot(a, w2_ref[...], preferred_element_type=F32)

    @pl.when(f == pl.num_programs(1) - 1)
    def _():
        def body(r, c):
            sl = pl.ds(pl.multiple_of(r * rows, rows), rows)
            y = acc_ref[sl, :]
            ms = jnp.mean(y * y, axis=-1, keepdims=True)
            y_ref[sl, :] = h_ref[sl, :] + (y * lax.rsqrt(ms + EPS)) * gpost_ref[...]
            return c
        lax.fori_loop(0, tm // rows, body, 0)


def _ffn(h2d, g_pre, g_post, w1, w2, *, tm=512, tf=1024):
    T = h2d.shape[0]
    vec = pl.BlockSpec((1, D_MODEL), lambda i, f: (0, 0))
    return pl.pallas_call(
        functools.partial(_ffn_kernel, rows=128),
        grid=(T // tm, D_FF // tf),
        in_specs=[
            pl.BlockSpec((tm, D_MODEL), lambda i, f: (i, 0)),
            vec, vec,
            pl.BlockSpec((D_MODEL, tf), lambda i, f: (0, f)),
            pl.BlockSpec((tf, D_MODEL), lambda i, f: (f, 0)),
        ],
        out_specs=pl.BlockSpec((tm, D_MODEL), lambda i, f: (i, 0)),
        out_shape=jax.ShapeDtypeStruct((T, D_MODEL), F32),
        scratch_shapes=[pltpu.VMEM((tm, D_MODEL), BF16), pltpu.VMEM((tm, D_MODEL), F32)],
        compiler_params=_cparams(("parallel", "arbitrary"), 48),
        name="ffn",
    )(h2d, g_pre, g_post, w1, w2)


def _encoder_layer(x, wts):
    B, S, _ = x.shape
    x2d = x.reshape(B * S, D_MODEL)
    P, Z = _inproj(x2d, wts["g_mix_pre"], wts["w_main"], wts["w_z"])
    o_f, o_b = _gla(P, Z, wts["wa_f"], wts["ba_f"], wts["wa_b"], wts["ba_b"], B, S)
    out_b = _conv_module(P, wts["conv_w"], wts["conv_b"], wts["ln_g"], wts["ln_b"], wts["w_pw2"], B, S)
    h = _merge(o_f, o_b, P, out_b, x2d, wts["gla_norm"], wts["g_mix_post"], wts["w_out"])
    y = _ffn(h, wts["g_ffn_pre"], wts["g_ffn_post"], wts["w_ff1"], wts["w_ff2"])
    return y.reshape(B, S, D_MODEL)


def kernel(x_prompt, x_sample, norm_mix_pre, norm_mix_post, norm_ffn_pre, norm_ffn_post, w_in, w_a_fwd, b_a_fwd, w_a_bwd, b_a_bwd, gla_norm, conv_w, conv_b, conv_ln_g, conv_ln_b, w_pw2, w_out, w_ff1, w_ff2):
    z0 = 2 * D_K + 2 * D_V
    z1 = z0 + 2 * GATE_RANK
    row = lambda v: v.reshape(1, -1).astype(F32)
    w_z = jnp.zeros((D_MODEL, Z_COLS), F32).at[:, :2 * GATE_RANK].set(w_in[:, z0:z1])
    wa_f = jnp.zeros((Z_COLS, D_K), F32).at[:GATE_RANK].set(w_a_fwd)
    wa_b = jnp.zeros((Z_COLS, D_K), F32).at[GATE_RANK:2 * GATE_RANK].set(w_a_bwd)
    wts = {
        "g_mix_pre": row(norm_mix_pre), "g_mix_post": row(norm_mix_post),
        "g_ffn_pre": row(norm_ffn_pre), "g_ffn_post": row(norm_ffn_post),
        "w_main": jnp.concatenate([w_in[:, :z0], w_in[:, z1:]], axis=1).astype(BF16),
        "w_z": w_z.astype(BF16),
        "wa_f": wa_f.astype(BF16), "ba_f": row(b_a_fwd),
        "wa_b": wa_b.astype(BF16), "ba_b": row(b_a_bwd),
        "gla_norm": row(gla_norm),
        "conv_w": conv_w, "conv_b": row(conv_b),
        "ln_g": row(conv_ln_g), "ln_b": row(conv_ln_b),
        "w_pw2": w_pw2.astype(BF16), "w_out": w_out.astype(BF16),
        "w_ff1": w_ff1.astype(BF16), "w_ff2": w_ff2.astype(BF16),
    }
    return (_encoder_layer(x_prompt, wts), _encoder_layer(x_sample, wts))
```

```python
import functools

import jax
import jax.numpy as jnp
from jax import lax
from jax.experimental import pallas as pl
from jax.experimental.pallas import tpu as pltpu

F32 = jnp.float32
BF16 = jnp.bfloat16

D_MODEL = 2048
N_HEADS = 4
D_K = D_MODEL // 2
D_V = D_MODEL
HEAD_DK = D_K // N_HEADS
HEAD_DV = D_V // N_HEADS
GATE_RANK = 16
GATE_TAU = 16.0
CHUNK = 64
CONV_WIDTH = 31
CONV_PAD = (CONV_WIDTH - 1) // 2
D_FF = 4 * D_MODEL
EPS = 1e-6

P_COLS = 7 * D_MODEL
OFF_G = 2 * D_MODEL
OFF_PWA = 3 * D_MODEL
OFF_PWB = 4 * D_MODEL
OFF_GA = 5 * D_MODEL
OFF_GB = 6 * D_MODEL
Z_COLS = 128

HALO = 16
GLA_BLOCK = 2 * CHUNK

_NT = (((1,), (1,)), ((), ()))
_TN = (((0,), (0,)), ((), ()))

_MIB = 1024 * 1024


def _cparams(sem, vmem_mib):
    return pltpu.CompilerParams(dimension_semantics=sem, vmem_limit_bytes=vmem_mib * _MIB)


def _sigmoid(x):
    return 1.0 / (1.0 + jnp.exp(-x))


def _log_sigmoid(x):
    return jnp.minimum(x, 0.0) - jnp.log(1.0 + jnp.exp(-jnp.abs(x)))


def _inproj_kernel(x_ref, g_ref, w_ref, wz_ref, p_ref, z_ref, u_ref, *, rows):
    @pl.when(pl.program_id(1) == 0)
    def _():
        def body(r, c):
            sl = pl.ds(pl.multiple_of(r * rows, rows), rows)
            x = x_ref[sl, :]
            ms = jnp.mean(x * x, axis=-1, keepdims=True)
            u = (x * lax.rsqrt(ms + EPS)) * g_ref[...]
            u_ref[sl, :] = u.astype(BF16)
            return c
        lax.fori_loop(0, x_ref.shape[0] // rows, body, 0)
        z_ref[...] = jnp.dot(u_ref[...], wz_ref[...], preferred_element_type=F32)

    p_ref[...] = jnp.dot(u_ref[...], w_ref[...], preferred_element_type=F32).astype(p_ref.dtype)


def _inproj(x2d, g_pre, w_main, w_z, *, tm=1024, tn=2048):
    T = x2d.shape[0]
    grid = (T // tm, P_COLS // tn)
    return pl.pallas_call(
        functools.partial(_inproj_kernel, rows=128),
        grid=grid,
        in_specs=[
            pl.BlockSpec((tm, D_MODEL), lambda i, j: (i, 0)),
            pl.BlockSpec((1, D_MODEL), lambda i, j: (0, 0)),
            pl.BlockSpec((D_MODEL, tn), lambda i, j: (0, j)),
            pl.BlockSpec((D_MODEL, Z_COLS), lambda i, j: (0, 0)),
        ],
        out_specs=[
            pl.BlockSpec((tm, tn), lambda i, j: (i, j)),
            pl.BlockSpec((tm, Z_COLS), lambda i, j: (i, 0)),
        ],
        out_shape=[
            jax.ShapeDtypeStruct((T, P_COLS), BF16),
            jax.ShapeDtypeStruct((T, Z_COLS), F32),
        ],
        scratch_shapes=[pltpu.VMEM((tm, D_MODEL), BF16)],
        compiler_params=_cparams(("parallel", "arbitrary"), 52),
        name="inproj",
    )(x2d, g_pre, w_main, w_z)


def _gla_kernel(qf_ref, kf_ref, vf_ref, zf_ref, qb_ref, kb_ref, vb_ref, zb_ref,
                waf_ref, baf_ref, wab_ref, bab_ref, of_ref, ob_ref,
                st_ref, la_ref, b_ref, qk_ref, x_ref, d_ref, p_ref, oi_ref):
    @pl.when(pl.program_id(1) == 0)
    def _():
        st_ref[...] = jnp.zeros_like(st_ref)

    def log_decay(z_ref, w_ref, bias_ref):
        pre = jnp.dot(z_ref[...].astype(BF16), w_ref[...], preferred_element_type=F32) + bias_ref[...]
        return _log_sigmoid(pre) / GATE_TAU

    la_ref[0] = log_decay(zf_ref, waf_ref, baf_ref)
    la_ref[1] = log_decay(zb_ref, wab_ref, bab_ref)

    C = CHUNK
    row3 = lax.broadcasted_iota(jnp.int32, (C, 3 * C), 0)
    col3 = lax.broadcasted_iota(jnp.int32, (C, 3 * C), 1) & (C - 1)
    row = lax.broadcasted_iota(jnp.int32, (C, C), 0)
    col = lax.broadcasted_iota(jnp.int32, (C, C), 1)
    dirs = (
        (qf_ref, kf_ref, vf_ref, of_ref, (col3 <= row3).astype(BF16), col <= row, C // 2, C - 1, 0),
        (qb_ref, kb_ref, vb_ref, ob_ref, (col3 >= row3).astype(BF16), col >= row, C - 1 - C // 2, 0, 1),
    )
    n_blocks = qf_ref.shape[0] // GLA_BLOCK
    chains = [(dr, hd) for dr in range(2) for hd in range(N_HEADS)]
    klanes = lambda hd: slice(hd * HEAD_DK, (hd + 1) * HEAD_DK)
    vlanes = lambda hd: slice(hd * HEAD_DV, (hd + 1) * HEAD_DV)
    sub = lambda sc: slice(sc * C, (sc + 1) * C)
    zeros_cc = jnp.zeros((C, C), F32)

    def body(c, carry):
        rows = (pl.ds(pl.multiple_of(c * GLA_BLOCK, GLA_BLOCK), GLA_BLOCK),
                pl.ds(pl.multiple_of((n_blocks - 1 - c) * GLA_BLOCK, GLA_BLOCK), GLA_BLOCK))

        for dr in range(2):
            tri3, last_idx, first = dirs[dr][4], dirs[dr][7], dirs[dr][8]
            la = la_ref[dr, rows[dr], :]
            h1 = la.astype(BF16)
            r1 = la - h1.astype(F32)
            h2 = r1.astype(BF16)
            h3 = (r1 - h2.astype(F32)).astype(BF16)
            total = []
            for sc in range(2):
                b = jnp.dot(tri3, jnp.concatenate([h1[sub(sc)], h2[sub(sc)], h3[sub(sc)]], axis=0),
                            preferred_element_type=F32)
                b_ref[dr, sub(sc), :] = b
                total.append(b[last_idx:last_idx + 1, :])
            d_ref[dr, 0:1, :] = jnp.exp(total[0] + total[1])
            d_ref[dr, 1:2, :] = jnp.exp(total[first])
            d_ref[dr, 2:3, :] = jnp.exp(total[1 - first])

        for dr, hd in chains:
            q_ref, k_ref, ref_idx, last_idx, first = (dirs[dr][0], dirs[dr][1], dirs[dr][6],
                                                      dirs[dr][7], dirs[dr][8])
            kl = klanes(hd)
            b = b_ref[dr, :, kl]

            def per_chunk(idx):
                return jnp.concatenate([jnp.broadcast_to(b[sc * C + idx:sc * C + idx + 1, :], (C, HEAD_DK))
                                        for sc in range(2)], axis=0)

            b_mid, b_last = per_chunk(ref_idx), per_chunk(last_idx)
            q = q_ref[rows[dr], kl].astype(F32) * (HEAD_DK ** -0.5)
            k = k_ref[rows[dr], kl].astype(F32)
            q_loc = q * jnp.exp(b)
            k_loc = k * jnp.exp(b_last - b)
            qk_ref[dr, 1, :, kl] = (q * jnp.exp(b - b_mid)).astype(BF16)
            qk_ref[dr, 2, :, kl] = (k * jnp.exp(b_mid - b)).astype(BF16)
            early, late = sub(first), sub(1 - first)
            qk_ref[dr, 0, early, kl] = q_loc[early].astype(BF16)
            qk_ref[dr, 0, late, kl] = (q_loc[late] * d_ref[dr, 1:2, kl]).astype(BF16)
            qk_ref[dr, 3, early, kl] = (k_loc[early] * d_ref[dr, 2:3, kl]).astype(BF16)
            qk_ref[dr, 3, late, kl] = k_loc[late].astype(BF16)
            x_ref[dr, 0, :, kl] = q_loc[late].astype(BF16)
            x_ref[dr, 1, :, kl] = k_loc[early].astype(BF16)

        for dr, hd in chains:
            mask, first = dirs[dr][5], dirs[dr][8]
            kl = klanes(hd)
            diag = []
            for sc in range(2):
                s = lax.dot_general(qk_ref[dr, 1, sub(sc), kl], qk_ref[dr, 2, sub(sc), kl], _NT,
                                    preferred_element_type=F32)
                diag.append(jnp.where(mask, s, 0.0))
            cross = lax.dot_general(x_ref[dr, 0, :, kl], x_ref[dr, 1, :, kl], _NT, preferred_element_type=F32)
            if first == 0:
                top = jnp.concatenate([diag[0], zeros_cc], axis=1)
                bottom = jnp.concatenate([cross, diag[1]], axis=1)
            else:
                top = jnp.concatenate([diag[0], cross], axis=1)
                bottom = jnp.concatenate([zeros_cc, diag[1]], axis=1)
            p_ref[dr * N_HEADS + hd] = jnp.concatenate([top, bottom], axis=0).astype(BF16)

        for dr, hd in chains:
            v_ref = dirs[dr][2]
            ch = dr * N_HEADS + hd
            st = st_ref[ch]
            oi_ref[ch] = lax.dot_general(qk_ref[dr, 0, :, klanes(hd)], st.astype(BF16), _NT,
                                         preferred_element_type=F32)
            upd = lax.dot_general(v_ref[rows[dr], vlanes(hd)], qk_ref[dr, 3, :, klanes(hd)], _TN,
                                  preferred_element_type=F32)
            st_ref[ch] = d_ref[dr, 0:1, klanes(hd)] * st + upd

        for dr, hd in chains:
            v_ref, o_ref = dirs[dr][2], dirs[dr][3]
            ch = dr * N_HEADS + hd
            o = oi_ref[ch] + jnp.dot(p_ref[ch], v_ref[rows[dr], vlanes(hd)], preferred_element_type=F32)
            o_ref[rows[dr], vlanes(hd)] = o.astype(o_ref.dtype)
        return carry

    lax.fori_loop(0, n_blocks, body, 0)


def _gla(P, Z, wa_f, ba_f, wa_b, ba_b, B, S, *, ts=512):
    T = B * S
    nb = S // ts
    fwd = lambda b, i: b * nb + i
    bwd = lambda b, i: b * nb + (nb - 1 - i)

    def stream_specs(rowmap):
        return [
            pl.BlockSpec((ts, D_K), lambda b, i: (rowmap(b, i), 0)),
            pl.BlockSpec((ts, D_K), lambda b, i: (rowmap(b, i), 1)),
            pl.BlockSpec((ts, D_V), lambda b, i: (rowmap(b, i), 1)),
            pl.BlockSpec((ts, Z_COLS), lambda b, i: (rowmap(b, i), 0)),
        ]

    w_spec = pl.BlockSpec((Z_COLS, D_K), lambda b, i: (0, 0))
    b_spec = pl.BlockSpec((1, D_K), lambda b, i: (0, 0))
    return pl.pallas_call(
        _gla_kernel,
        grid=(B, nb),
        in_specs=stream_specs(fwd) + stream_specs(bwd) + [w_spec, b_spec, w_spec, b_spec],
        out_specs=[
            pl.BlockSpec((ts, D_V), lambda b, i: (fwd(b, i), 0)),
            pl.BlockSpec((ts, D_V), lambda b, i: (bwd(b, i), 0)),
        ],
        out_shape=[jax.ShapeDtypeStruct((T, D_V), BF16)] * 2,
        scratch_shapes=[
            pltpu.VMEM((2 * N_HEADS, HEAD_DV, HEAD_DK), F32),
            pltpu.VMEM((2, ts, D_K), F32),
            pltpu.VMEM((2, GLA_BLOCK, D_K), F32),
            pltpu.VMEM((2, 4, GLA_BLOCK, D_K), BF16),
            pltpu.VMEM((2, 2, CHUNK, D_K), BF16),
            pltpu.VMEM((2, 8, D_K), F32),
            pltpu.VMEM((2 * N_HEADS, GLA_BLOCK, GLA_BLOCK), BF16),
            pltpu.VMEM((2 * N_HEADS, GLA_BLOCK, HEAD_DV), F32),
        ],
        compiler_params=_cparams(("parallel", "arbitrary"), 56),
        name="gla",
    )(P, P, P, Z, P, P, P, Z, wa_f, ba_f, wa_b, ba_b)


def _conv_kernel(pa_ref, pb_ref, pa_prev, pb_prev, pa_next, pb_next,
                 cw_ref, cb_ref, lg_ref, lb_ref, w2_ref, out_ref,
                 glu_ref, dw_ref, act_ref, *, rb, lb, nt):
    s = pl.program_id(0)
    tile = jnp.minimum(s, pl.num_programs(0) - 2)
    i = tile % nt
    slot = s % 2
    tt = pa_ref.shape[0]

    @pl.when(s == 0)
    def _():
        act_ref[1] = jnp.zeros(act_ref.shape[1:], act_ref.dtype)

    def glu(a, b):
        return a.astype(F32) * _sigmoid(b.astype(F32))

    n_win = tt // rb
    head = rb + HALO
    glu_ref[0, 0:HALO, :] = jnp.where(i > 0, glu(pa_prev[...], pb_prev[...]), 0.0)
    glu_ref[0, HALO:HALO + head, :] = glu(pa_ref[0:head, :], pb_ref[0:head, :])
    glu_ref[n_win - 1, 0:head, :] = glu(pa_ref[tt - head:tt, :], pb_ref[tt - head:tt, :])
    glu_ref[n_win - 1, head:head + HALO, :] = jnp.where(i < nt - 1, glu(pa_next[...], pb_next[...]), 0.0)

    def fill_body(m, c):
        src = pl.ds(pl.multiple_of(m * rb - HALO, HALO), rb + 2 * HALO)
        glu_ref[m] = glu(pa_ref[src, :], pb_ref[src, :])
        return c

    lax.fori_loop(1, n_win - 1, fill_body, 0)

    sub = 8
    shift = HALO - CONV_PAD
    n_col = D_MODEL // n_win

    for m in range(n_win):
        cols = slice(m * n_col, (m + 1) * n_col)
        out_ref[:, cols] = jnp.dot(act_ref[1 - slot], w2_ref[:, cols],
                                   preferred_element_type=F32).astype(out_ref.dtype)
        out_rows = slice(m * rb, (m + 1) * rb)
        for l0 in range(0, D_MODEL, lb):
            lanes = slice(l0, l0 + lb)
            acc = None
            for ph in range(sub):
                part = None
                for q in range((CONV_WIDTH + shift) // sub + 1):
                    j = sub * q + ph - shift
                    if 0 <= j < CONV_WIDTH:
                        term = cw_ref[j:j + 1, lanes] * glu_ref[m, pl.ds(sub * q, rb + sub), lanes]
                        part = term if part is None else part + term
                shifted = part[ph:ph + rb]
                acc = shifted if acc is None else acc + shifted
            dw_ref[out_rows, lanes] = acc + cb_ref[:, lanes]

    ln_rows = 64

    def ln_body(r, c):
        sl = pl.ds(pl.multiple_of(r * ln_rows, ln_rows), ln_rows)
        x = dw_ref[sl, :]
        mu = jnp.mean(x, axis=-1, keepdims=True)
        xc = x - mu
        var = jnp.mean(xc * xc, axis=-1, keepdims=True)
        y = xc * lax.rsqrt(var + EPS) * lg_ref[...] + lb_ref[...]
        act_ref[slot, sl, :] = (y * _sigmoid(y)).astype(BF16)
        return c

    lax.fori_loop(0, tt // ln_rows, ln_body, 0)


def _conv_module(P, conv_w, conv_b, ln_g, ln_b, w_pw2, B, S, *, tt=512):
    T = B * S
    nt = S // tt
    n_tiles = B * nt
    hb = tt // HALO
    n_halo_blocks = T // HALO
    ca, cb = OFF_PWA // D_MODEL, OFF_PWB // D_MODEL
    tile = lambda s: jnp.minimum(s, n_tiles - 1)
    prev = lambda s: jnp.maximum(tile(s) * hb - 1, 0)
    nxt = lambda s: jnp.minimum((tile(s) + 1) * hb, n_halo_blocks - 1)
    vec = lambda n: pl.BlockSpec((n, D_MODEL), lambda s: (0, 0))
    rb = 64
    return pl.pallas_call(
        functools.partial(_conv_kernel, rb=rb, lb=128, nt=nt),
        grid=(n_tiles + 1,),
        in_specs=[
            pl.BlockSpec((tt, D_MODEL), lambda s: (tile(s), ca)),
            pl.BlockSpec((tt, D_MODEL), lambda s: (tile(s), cb)),
            pl.BlockSpec((HALO, D_MODEL), lambda s: (prev(s), ca)),
            pl.BlockSpec((HALO, D_MODEL), lambda s: (prev(s), cb)),
            pl.BlockSpec((HALO, D_MODEL), lambda s: (nxt(s), ca)),
            pl.BlockSpec((HALO, D_MODEL), lambda s: (nxt(s), cb)),
            vec(CONV_WIDTH), vec(1), vec(1), vec(1),
            pl.BlockSpec((D_MODEL, D_MODEL), lambda s: (0, 0), pipeline_mode=pl.Buffered(1)),
        ],
        out_specs=pl.BlockSpec((tt, D_MODEL), lambda s: (jnp.maximum(s - 1, 0), 0)),
        out_shape=jax.ShapeDtypeStruct((T, D_MODEL), BF16),
        scratch_shapes=[
            pltpu.VMEM((tt // rb, rb + 2 * HALO, D_MODEL), F32),
            pltpu.VMEM((tt, D_MODEL), F32),
            pltpu.VMEM((2, tt, D_MODEL), BF16),
        ],
        compiler_params=_cparams(("arbitrary",), 52),
        name="conv_module",
    )(P, P, P, P, P, P, conv_w, conv_b, ln_g, ln_b, w_pw2)


def _merge_kernel(of_ref, ob_ref, g_ref, ga_ref, gb_ref, cb_ref, x_ref, gn_ref, gpost_ref, wo_ref,
                  h_ref, m_ref, *, rows, norm_rows):
    s = pl.program_id(0)
    slot = s % 2
    tt = x_ref.shape[0]
    n_blk = tt // rows
    n_col = D_MODEL // n_blk

    @pl.when(s == 0)
    def _():
        m_ref[1] = jnp.zeros(m_ref.shape[1:], m_ref.dtype)

    for r in range(n_blk):
        cols = slice(r * n_col, (r + 1) * n_col)
        h_ref[:, cols] = jnp.dot(m_ref[1 - slot], wo_ref[:, cols], preferred_element_type=F32)
        sl = slice(r * rows, (r + 1) * rows)
        o = of_ref[sl, :].astype(F32) + ob_ref[sl, :].astype(F32)
        heads = []
        for hd in range(N_HEADS):
            oh = o[:, hd * HEAD_DV:(hd + 1) * HEAD_DV]
            ms = jnp.mean(oh * oh, axis=-1, keepdims=True)
            heads.append(oh * lax.rsqrt(ms + EPS))
        on = jnp.concatenate(heads, axis=-1) * gn_ref[...]
        g = g_ref[sl, :].astype(F32)
        out_a = on * (g * _sigmoid(g))
        merged = (_sigmoid(ga_ref[sl, :].astype(F32)) * out_a
                  + _sigmoid(gb_ref[sl, :].astype(F32)) * cb_ref[sl, :].astype(F32))
        m_ref[slot, sl, :] = merged.astype(BF16)

    def norm_body(r, c):
        sl = pl.ds(pl.multiple_of(r * norm_rows, norm_rows), norm_rows)
        y = h_ref[sl, :]
        ms = jnp.mean(y * y, axis=-1, keepdims=True)
        h_ref[sl, :] = x_ref[sl, :] + (y * lax.rsqrt(ms + EPS)) * gpost_ref[...]
        return c

    lax.fori_loop(0, tt // norm_rows, norm_body, 0)


def _merge(o_f, o_b, P, out_b, x2d, gla_norm, g_post, w_out, *, tt=512):
    T = x2d.shape[0]
    n_tiles = T // tt
    cur = lambda c: pl.BlockSpec((tt, D_MODEL), lambda s: (jnp.minimum(s, n_tiles - 1), c))
    lag = pl.BlockSpec((tt, D_MODEL), lambda s: (jnp.maximum(s - 1, 0), 0))
    vec = pl.BlockSpec((1, D_MODEL), lambda s: (0, 0))
    return pl.pallas_call(
        functools.partial(_merge_kernel, rows=64, norm_rows=64),
        grid=(n_tiles + 1,),
        in_specs=[
            cur(0), cur(0),
            cur(OFF_G // D_MODEL), cur(OFF_GA // D_MODEL), cur(OFF_GB // D_MODEL),
            cur(0), lag, vec, vec,
            pl.BlockSpec((D_MODEL, D_MODEL), lambda s: (0, 0), pipeline_mode=pl.Buffered(1)),
        ],
        out_specs=lag,
        out_shape=jax.ShapeDtypeStruct((T, D_MODEL), F32),
        scratch_shapes=[pltpu.VMEM((2, tt, D_MODEL), BF16)],
        compiler_params=_cparams(("arbitrary",), 56),
        name="merge_outproj",
    )(o_f, o_b, P, P, P, out_b, x2d, gla_norm, g_post, w_out)


def _ffn_kernel(h_ref, gpre_ref, gpost_ref, w1_ref, w2_ref, y_ref, u_ref, *, rows):
    acc_ref = y_ref
    f = pl.program_id(1)
    tm = h_ref.shape[0]

    @pl.when(f == 0)
    def _():
        def body(r, c):
            sl = pl.ds(pl.multiple_of(r * rows, rows), rows)
            x = h_ref[sl, :]
            ms = jnp.mean(x * x, axis=-1, keepdims=True)
            u_ref[sl, :] = ((x * lax.rsqrt(ms + EPS)) * gpre_ref[...]).astype(BF16)
            return c
        lax.fori_loop(0, tm // rows, body, 0)
        acc_ref[...] = jnp.zeros_like(acc_ref)

    a = jnp.dot(u_ref[...], w1_ref[...], preferred_element_type=F32)
    a = jnp.maximum(a, 0.0)
    a = (a * a).astype(BF16)
    acc_ref[...] += jnp.dot(a, w2_ref[...], preferred_element_type=F32)

    @pl.when(f == pl.num_programs(1) - 1)
    def _():
        def body(r, c):
            sl = pl.ds(pl.multiple_of(r * rows, rows), rows)
            y = acc_ref[sl, :]
            ms = jnp.mean(y * y, axis=-1, keepdims=True)
            y_ref[sl, :] = h_ref[sl, :] + (y * lax.rsqrt(ms + EPS)) * gpost_ref[...]
            return c
        lax.fori_loop(0, tm // rows, body, 0)


def _ffn(h2d, g_pre, g_post, w1, w2, *, tm=1024, tf=1024):
    T = h2d.shape[0]
    vec = pl.BlockSpec((1, D_MODEL), lambda i, f: (0, 0))
    return pl.pallas_call(
        functools.partial(_ffn_kernel, rows=128),
        grid=(T // tm, D_FF // tf),
        in_specs=[
            pl.BlockSpec((tm, D_MODEL), lambda i, f: (i, 0)),
            vec, vec,
            pl.BlockSpec((D_MODEL, tf), lambda i, f: (0, f)),
            pl.BlockSpec((tf, D_MODEL), lambda i, f: (f, 0)),
        ],
        out_specs=pl.BlockSpec((tm, D_MODEL), lambda i, f: (i, 0)),
        out_shape=jax.ShapeDtypeStruct((T, D_MODEL), F32),
        scratch_shapes=[pltpu.VMEM((tm, D_MODEL), BF16)],
        compiler_params=_cparams(("parallel", "arbitrary"), 58),
        name="ffn",
    )(h2d, g_pre, g_post, w1, w2)


def _encoder_layer(x, wts):
    B, S, _ = x.shape
    x2d = x.reshape(B * S, D_MODEL)
    P, Z = _inproj(x2d, wts["g_mix_pre"], wts["w_main"], wts["w_z"])
    o_f, o_b = _gla(P, Z, wts["wa_f"], wts["ba_f"], wts["wa_b"], wts["ba_b"], B, S)
    out_b = _conv_module(P, wts["conv_w"], wts["conv_b"], wts["ln_g"], wts["ln_b"], wts["w_pw2"], B, S)
    h = _merge(o_f, o_b, P, out_b, x2d, wts["gla_norm"], wts["g_mix_post"], wts["w_out"])
    y = _ffn(h, wts["g_ffn_pre"], wts["g_ffn_post"], wts["w_ff1"], wts["w_ff2"])
    return y.reshape(B, S, D_MODEL)


def kernel(x_prompt, x_sample, norm_mix_pre, norm_mix_post, norm_ffn_pre, norm_ffn_post, w_in, w_a_fwd, b_a_fwd, w_a_bwd, b_a_bwd, gla_norm, conv_w, conv_b, conv_ln_g, conv_ln_b, w_pw2, w_out, w_ff1, w_ff2):
    z0 = 2 * D_K + 2 * D_V
    z1 = z0 + 2 * GATE_RANK
    row = lambda v: v.reshape(1, -1).astype(F32)
    w_z = jnp.zeros((D_MODEL, Z_COLS), F32).at[:, :2 * GATE_RANK].set(w_in[:, z0:z1])
    wa_f = jnp.zeros((Z_COLS, D_K), F32).at[:GATE_RANK].set(w_a_fwd)
    wa_b = jnp.zeros((Z_COLS, D_K), F32).at[GATE_RANK:2 * GATE_RANK].set(w_a_bwd)
    wts = {
        "g_mix_pre": row(norm_mix_pre), "g_mix_post": row(norm_mix_post),
        "g_ffn_pre": row(norm_ffn_pre), "g_ffn_post": row(norm_ffn_post),
        "w_main": jnp.concatenate([w_in[:, :z0], w_in[:, z1:]], axis=1).astype(BF16),
        "w_z": w_z.astype(BF16),
        "wa_f": wa_f.astype(BF16), "ba_f": row(b_a_fwd),
        "wa_b": wa_b.astype(BF16), "ba_b": row(b_a_bwd),
        "gla_norm": row(gla_norm),
        "conv_w": conv_w, "conv_b": row(conv_b),
        "ln_g": row(conv_ln_g), "ln_b": row(conv_ln_b),
        "w_pw2": w_pw2.astype(BF16), "w_out": w_out.astype(BF16),
        "w_ff1": w_ff1.astype(BF16), "w_ff2": w_ff2.astype(BF16),
    }
    return (_encoder_layer(x_prompt, wts), _encoder_layer(x_sample, wts))
```

```python
import functools

import jax
import jax.numpy as jnp
from jax import lax
from jax.experimental import pallas as pl
from jax.experimental.pallas import tpu as pltpu

F32 = jnp.float32
BF16 = jnp.bfloat16

D_MODEL = 2048
N_HEADS = 4
D_K = D_MODEL // 2
D_V = D_MODEL
HEAD_DK = D_K // N_HEADS
HEAD_DV = D_V // N_HEADS
GATE_RANK = 16
GATE_TAU = 16.0
CHUNK = 64
CONV_WIDTH = 31
CONV_PAD = (CONV_WIDTH - 1) // 2
D_FF = 4 * D_MODEL
EPS = 1e-6

P_COLS = 7 * D_MODEL
OFF_G = 2 * D_MODEL
OFF_PWA = 3 * D_MODEL
OFF_PWB = 4 * D_MODEL
OFF_GA = 5 * D_MODEL
OFF_GB = 6 * D_MODEL
Z_COLS = 128

HALO = 16
GLA_BLOCK = 2 * CHUNK

_NT = (((1,), (1,)), ((), ()))
_TN = (((0,), (0,)), ((), ()))

_MIB = 1024 * 1024


def _cparams(sem, vmem_mib):
    return pltpu.CompilerParams(dimension_semantics=sem, vmem_limit_bytes=vmem_mib * _MIB)


def _sigmoid(x):
    return 1.0 / (1.0 + jnp.exp(-x))


def _log_sigmoid(x):
    return jnp.minimum(x, 0.0) - jnp.log(1.0 + jnp.exp(-jnp.abs(x)))


def _inproj_kernel(x_ref, g_ref, w_ref, wz_ref, p_ref, z_ref, u_ref, *, rows):
    @pl.when(pl.program_id(1) == 0)
    def _():
        def body(r, c):
            sl = pl.ds(pl.multiple_of(r * rows, rows), rows)
            x = x_ref[sl, :]
            ms = jnp.mean(x * x, axis=-1, keepdims=True)
            u = (x * lax.rsqrt(ms + EPS)) * g_ref[...]
            u_ref[sl, :] = u.astype(BF16)
            return c
        lax.fori_loop(0, x_ref.shape[0] // rows, body, 0)
        z_ref[...] = jnp.dot(u_ref[...], wz_ref[...], preferred_element_type=F32)

    p_ref[...] = jnp.dot(u_ref[...], w_ref[...], preferred_element_type=F32).astype(p_ref.dtype)


def _inproj(x2d, g_pre, w_main, w_z, *, tm=1024, tn=2048):
    T = x2d.shape[0]
    grid = (T // tm, P_COLS // tn)
    return pl.pallas_call(
        functools.partial(_inproj_kernel, rows=256),
        grid=grid,
        in_specs=[
            pl.BlockSpec((tm, D_MODEL), lambda i, j: (i, 0)),
            pl.BlockSpec((1, D_MODEL), lambda i, j: (0, 0)),
            pl.BlockSpec((D_MODEL, tn), lambda i, j: (0, j)),
            pl.BlockSpec((D_MODEL, Z_COLS), lambda i, j: (0, 0)),
        ],
        out_specs=[
            pl.BlockSpec((tm, tn), lambda i, j: (i, j)),
            pl.BlockSpec((tm, Z_COLS), lambda i, j: (i, 0)),
        ],
        out_shape=[
            jax.ShapeDtypeStruct((T, P_COLS), BF16),
            jax.ShapeDtypeStruct((T, Z_COLS), F32),
        ],
        scratch_shapes=[pltpu.VMEM((tm, D_MODEL), BF16)],
        compiler_params=_cparams(("parallel", "arbitrary"), 52),
        name="inproj",
    )(x2d, g_pre, w_main, w_z)


def _gla_kernel(qf_ref, kf_ref, vf_ref, zf_ref, qb_ref, kb_ref, vb_ref, zb_ref,
                waf_ref, baf_ref, wab_ref, bab_ref, of_ref, ob_ref,
                st_ref, la_ref, b_ref, qk_ref, x_ref, d_ref, p_ref, oi_ref):
    @pl.when(pl.program_id(1) == 0)
    def _():
        st_ref[...] = jnp.zeros_like(st_ref)

    def log_decay(z_ref, w_ref, bias_ref):
        pre = jnp.dot(z_ref[...].astype(BF16), w_ref[...], preferred_element_type=F32) + bias_ref[...]
        return _log_sigmoid(pre) / GATE_TAU

    la_ref[0] = log_decay(zf_ref, waf_ref, baf_ref)
    la_ref[1] = log_decay(zb_ref, wab_ref, bab_ref)

    C = CHUNK
    row3 = lax.broadcasted_iota(jnp.int32, (C, 3 * C), 0)
    col3 = lax.broadcasted_iota(jnp.int32, (C, 3 * C), 1) & (C - 1)
    row = lax.broadcasted_iota(jnp.int32, (C, C), 0)
    col = lax.broadcasted_iota(jnp.int32, (C, C), 1)
    dirs = (
        (qf_ref, kf_ref, vf_ref, of_ref, (col3 <= row3).astype(BF16), col <= row, C // 2, C - 1, 0),
        (qb_ref, kb_ref, vb_ref, ob_ref, (col3 >= row3).astype(BF16), col >= row, C - 1 - C // 2, 0, 1),
    )
    n_blocks = qf_ref.shape[0] // GLA_BLOCK
    chains = [(dr, hd) for dr in range(2) for hd in range(N_HEADS)]
    klanes = lambda hd: slice(hd * HEAD_DK, (hd + 1) * HEAD_DK)
    vlanes = lambda hd: slice(hd * HEAD_DV, (hd + 1) * HEAD_DV)
    sub = lambda sc: slice(sc * C, (sc + 1) * C)
    zeros_cc = jnp.zeros((C, C), F32)

    def body(c, carry):
        rows = (pl.ds(pl.multiple_of(c * GLA_BLOCK, GLA_BLOCK), GLA_BLOCK),
                pl.ds(pl.multiple_of((n_blocks - 1 - c) * GLA_BLOCK, GLA_BLOCK), GLA_BLOCK))

        for dr in range(2):
            tri3, last_idx, first = dirs[dr][4], dirs[dr][7], dirs[dr][8]
            la = la_ref[dr, rows[dr], :]
            h1 = la.astype(BF16)
            r1 = la - h1.astype(F32)
            h2 = r1.astype(BF16)
            h3 = (r1 - h2.astype(F32)).astype(BF16)
            total = []
            for sc in range(2):
                b = jnp.dot(tri3, jnp.concatenate([h1[sub(sc)], h2[sub(sc)], h3[sub(sc)]], axis=0),
                            preferred_element_type=F32)
                b_ref[dr, sub(sc), :] = b
                total.append(b[last_idx:last_idx + 1, :])
            d_ref[dr, 0:1, :] = jnp.exp(total[0] + total[1])
            d_ref[dr, 1:2, :] = jnp.exp(total[first])
            d_ref[dr, 2:3, :] = jnp.exp(total[1 - first])

        for dr, hd in chains:
            q_ref, k_ref, ref_idx, last_idx, first = (dirs[dr][0], dirs[dr][1], dirs[dr][6],
                                                      dirs[dr][7], dirs[dr][8])
            kl = klanes(hd)
            b = b_ref[dr, :, kl]

            def per_chunk(idx):
                return jnp.concatenate([jnp.broadcast_to(b[sc * C + idx:sc * C + idx + 1, :], (C, HEAD_DK))
                                        for sc in range(2)], axis=0)

            b_mid, b_last = per_chunk(ref_idx), per_chunk(last_idx)
            q = q_ref[rows[dr], kl].astype(F32) * (HEAD_DK ** -0.5)
            k = k_ref[rows[dr], kl].astype(F32)
            q_loc = q * jnp.exp(b)
            k_loc = k * jnp.exp(b_last - b)
            qk_ref[dr, 1, :, kl] = (q * jnp.exp(b - b_mid)).astype(BF16)
            qk_ref[dr, 2, :, kl] = (k * jnp.exp(b_mid - b)).astype(BF16)
            early, late = sub(first), sub(1 - first)
            qk_ref[dr, 0, early, kl] = q_loc[early].astype(BF16)
            qk_ref[dr, 0, late, kl] = (q_loc[late] * d_ref[dr, 1:2, kl]).astype(BF16)
            qk_ref[dr, 3, early, kl] = (k_loc[early] * d_ref[dr, 2:3, kl]).astype(BF16)
            qk_ref[dr, 3, late, kl] = k_loc[late].astype(BF16)
            x_ref[dr, 0, :, kl] = q_loc[late].astype(BF16)
            x_ref[dr, 1, :, kl] = k_loc[early].astype(BF16)

        for dr, hd in chains:
            mask, first = dirs[dr][5], dirs[dr][8]
            kl = klanes(hd)
            diag = []
            for sc in range(2):
                s = lax.dot_general(qk_ref[dr, 1, sub(sc), kl], qk_ref[dr, 2, sub(sc), kl], _NT,
                                    preferred_element_type=F32)
                diag.append(jnp.where(mask, s, 0.0))
            cross = lax.dot_general(x_ref[dr, 0, :, kl], x_ref[dr, 1, :, kl], _NT, preferred_element_type=F32)
            if first == 0:
                top = jnp.concatenate([diag[0], zeros_cc], axis=1)
                bottom = jnp.concatenate([cross, diag[1]], axis=1)
            else:
                top = jnp.concatenate([diag[0], cross], axis=1)
                bottom = jnp.concatenate([zeros_cc, diag[1]], axis=1)
            p_ref[dr * N_HEADS + hd] = jnp.concatenate([top, bottom], axis=0).astype(BF16)

        for dr, hd in chains:
            v_ref = dirs[dr][2]
            ch = dr * N_HEADS + hd
            st = st_ref[ch]
            oi_ref[ch] = lax.dot_general(qk_ref[dr, 0, :, klanes(hd)], st.astype(BF16), _NT,
                                         preferred_element_type=F32)
            upd = lax.dot_general(v_ref[rows[dr], vlanes(hd)], qk_ref[dr, 3, :, klanes(hd)], _TN,
                                  preferred_element_type=F32)
            st_ref[ch] = d_ref[dr, 0:1, klanes(hd)] * st + upd

        for dr, hd in chains:
            v_ref, o_ref = dirs[dr][2], dirs[dr][3]
            ch = dr * N_HEADS + hd
            o = oi_ref[ch] + jnp.dot(p_ref[ch], v_ref[rows[dr], vlanes(hd)], preferred_element_type=F32)
            o_ref[rows[dr], vlanes(hd)] = o.astype(o_ref.dtype)
        return carry

    lax.fori_loop(0, n_blocks, body, 0)


def _gla(P, Z, wa_f, ba_f, wa_b, ba_b, B, S, *, ts=512):
    T = B * S
    nb = S // ts
    fwd = lambda b, i: b * nb + i
    bwd = lambda b, i: b * nb + (nb - 1 - i)

    def stream_specs(rowmap):
        return [
            pl.BlockSpec((ts, D_K), lambda b, i: (rowmap(b, i), 0)),
            pl.BlockSpec((ts, D_K), lambda b, i: (rowmap(b, i), 1)),
            pl.BlockSpec((ts, D_V), lambda b, i: (rowmap(b, i), 1)),
            pl.BlockSpec((ts, Z_COLS), lambda b, i: (rowmap(b, i), 0)),
        ]

    w_spec = pl.BlockSpec((Z_COLS, D_K), lambda b, i: (0, 0))
    b_spec = pl.BlockSpec((1, D_K), lambda b, i: (0, 0))
    return pl.pallas_call(
        _gla_kernel,
        grid=(B, nb),
        in_specs=stream_specs(fwd) + stream_specs(bwd) + [w_spec, b_spec, w_spec, b_spec],
        out_specs=[
            pl.BlockSpec((ts, D_V), lambda b, i: (fwd(b, i), 0)),
            pl.BlockSpec((ts, D_V), lambda b, i: (bwd(b, i), 0)),
        ],
        out_shape=[jax.ShapeDtypeStruct((T, D_V), BF16)] * 2,
        scratch_shapes=[
            pltpu.VMEM((2 * N_HEADS, HEAD_DV, HEAD_DK), F32),
            pltpu.VMEM((2, ts, D_K), F32),
            pltpu.VMEM((2, GLA_BLOCK, D_K), F32),
            pltpu.VMEM((2, 4, GLA_BLOCK, D_K), BF16),
            pltpu.VMEM((2, 2, CHUNK, D_K), BF16),
            pltpu.VMEM((2, 8, D_K), F32),
            pltpu.VMEM((2 * N_HEADS, GLA_BLOCK, GLA_BLOCK), BF16),
            pltpu.VMEM((2 * N_HEADS, GLA_BLOCK, HEAD_DV), F32),
        ],
        compiler_params=_cparams(("parallel", "arbitrary"), 56),
        name="gla",
    )(P, P, P, Z, P, P, P, Z, wa_f, ba_f, wa_b, ba_b)


def _conv_kernel(pa_ref, pb_ref, pa_prev, pb_prev, pa_next, pb_next,
                 cw_ref, cb_ref, lg_ref, lb_ref, w2_ref, out_ref,
                 glu_ref, dw_ref, act_ref, *, rb, lb, nt):
    s = pl.program_id(0)
    tile = jnp.minimum(s, pl.num_programs(0) - 2)
    i = tile % nt
    slot = s % 2
    tt = pa_ref.shape[0]

    @pl.when(s == 0)
    def _():
        act_ref[1] = jnp.zeros(act_ref.shape[1:], act_ref.dtype)

    def glu(a, b):
        return a.astype(F32) * _sigmoid(b.astype(F32))

    n_win = tt // rb
    body_rows = rb - HALO
    glu_ref[0, 0:HALO, :] = jnp.where(i > 0, glu(pa_prev[...], pb_prev[...]), 0.0)
    glu_ref[0, HALO:rb, :] = glu(pa_ref[0:body_rows, :], pb_ref[0:body_rows, :])
    glu_ref[n_win - 1, rb:rb + HALO, :] = glu(pa_ref[tt - HALO:tt, :], pb_ref[tt - HALO:tt, :])
    glu_ref[n_win - 1, rb + HALO:rb + 2 * HALO, :] = jnp.where(
        i < nt - 1, glu(pa_next[...], pb_next[...]), 0.0)

    def fill_body(m, c):
        src = pl.ds(pl.multiple_of(m * rb - HALO, HALO), rb)
        val = glu(pa_ref[src, :], pb_ref[src, :])
        glu_ref[m, 0:rb, :] = val
        glu_ref[m - 1, rb:rb + 2 * HALO, :] = val[0:2 * HALO]
        return c

    lax.fori_loop(1, n_win, fill_body, 0)

    sub = 8
    shift = HALO - CONV_PAD
    n_col = D_MODEL // n_win

    for m in range(n_win):
        cols = slice(m * n_col, (m + 1) * n_col)
        out_ref[:, cols] = jnp.dot(act_ref[1 - slot], w2_ref[:, cols],
                                   preferred_element_type=F32).astype(out_ref.dtype)
        out_rows = slice(m * rb, (m + 1) * rb)
        for l0 in range(0, D_MODEL, lb):
            lanes = slice(l0, l0 + lb)
            acc = None
            for ph in range(sub):
                part = None
                for q in range((CONV_WIDTH + shift) // sub + 1):
                    j = sub * q + ph - shift
                    if 0 <= j < CONV_WIDTH:
                        term = cw_ref[j:j + 1, lanes] * glu_ref[m, pl.ds(sub * q, rb + sub), lanes]
                        part = term if part is None else part + term
                shifted = part[ph:ph + rb]
                acc = shifted if acc is None else acc + shifted
            dw_ref[out_rows, lanes] = acc + cb_ref[:, lanes]

    ln_rows = 128

    def ln_body(r, c):
        sl = pl.ds(pl.multiple_of(r * ln_rows, ln_rows), ln_rows)
        x = dw_ref[sl, :]
        mu = jnp.mean(x, axis=-1, keepdims=True)
        xc = x - mu
        var = jnp.mean(xc * xc, axis=-1, keepdims=True)
        y = xc * lax.rsqrt(var + EPS) * lg_ref[...] + lb_ref[...]
        act_ref[slot, sl, :] = (y * _sigmoid(y)).astype(BF16)
        return c

    lax.fori_loop(0, tt // ln_rows, ln_body, 0)


def _conv_module(P, conv_w, conv_b, ln_g, ln_b, w_pw2, B, S, *, tt=512):
    T = B * S
    nt = S // tt
    n_tiles = B * nt
    hb = tt // HALO
    n_halo_blocks = T // HALO
    ca, cb = OFF_PWA // D_MODEL, OFF_PWB // D_MODEL
    tile = lambda s: jnp.minimum(s, n_tiles - 1)
    prev = lambda s: jnp.maximum(tile(s) * hb - 1, 0)
    nxt = lambda s: jnp.minimum((tile(s) + 1) * hb, n_halo_blocks - 1)
    vec = lambda n: pl.BlockSpec((n, D_MODEL), lambda s: (0, 0))
    rb = 64
    return pl.pallas_call(
        functools.partial(_conv_kernel, rb=rb, lb=128, nt=nt),
        grid=(n_tiles + 1,),
        in_specs=[
            pl.BlockSpec((tt, D_MODEL), lambda s: (tile(s), ca)),
            pl.BlockSpec((tt, D_MODEL), lambda s: (tile(s), cb)),
            pl.BlockSpec((HALO, D_MODEL), lambda s: (prev(s), ca)),
            pl.BlockSpec((HALO, D_MODEL), lambda s: (prev(s), cb)),
            pl.BlockSpec((HALO, D_MODEL), lambda s: (nxt(s), ca)),
            pl.BlockSpec((HALO, D_MODEL), lambda s: (nxt(s), cb)),
            vec(CONV_WIDTH), vec(1), vec(1), vec(1),
            pl.BlockSpec((D_MODEL, D_MODEL), lambda s: (0, 0), pipeline_mode=pl.Buffered(1)),
        ],
        out_specs=pl.BlockSpec((tt, D_MODEL), lambda s: (jnp.maximum(s - 1, 0), 0)),
        out_shape=jax.ShapeDtypeStruct((T, D_MODEL), BF16),
        scratch_shapes=[
            pltpu.VMEM((tt // rb, rb + 2 * HALO, D_MODEL), F32),
            pltpu.VMEM((tt, D_MODEL), F32),
            pltpu.VMEM((2, tt, D_MODEL), BF16),
        ],
        compiler_params=_cparams(("arbitrary",), 52),
        name="conv_module",
    )(P, P, P, P, P, P, conv_w, conv_b, ln_g, ln_b, w_pw2)


def _merge_kernel(of_ref, ob_ref, g_ref, ga_ref, gb_ref, cb_ref, x_ref, gn_ref, gpost_ref, wo_ref,
                  h_ref, m_ref, *, rows, norm_rows):
    s = pl.program_id(0)
    slot = s % 2
    tt = x_ref.shape[0]
    n_blk = tt // rows
    n_col = D_MODEL // n_blk

    @pl.when(s == 0)
    def _():
        m_ref[1] = jnp.zeros(m_ref.shape[1:], m_ref.dtype)

    for r in range(n_blk):
        cols = slice(r * n_col, (r + 1) * n_col)
        h_ref[:, cols] = jnp.dot(m_ref[1 - slot], wo_ref[:, cols], preferred_element_type=F32)
        sl = slice(r * rows, (r + 1) * rows)
        o = of_ref[sl, :].astype(F32) + ob_ref[sl, :].astype(F32)
        heads = []
        for hd in range(N_HEADS):
            oh = o[:, hd * HEAD_DV:(hd + 1) * HEAD_DV]
            ms = jnp.mean(oh * oh, axis=-1, keepdims=True)
            heads.append(oh * lax.rsqrt(ms + EPS))
        on = jnp.concatenate(heads, axis=-1) * gn_ref[...]
        g = g_ref[sl, :].astype(F32)
        out_a = on * (g * _sigmoid(g))
        merged = (_sigmoid(ga_ref[sl, :].astype(F32)) * out_a
                  + _sigmoid(gb_ref[sl, :].astype(F32)) * cb_ref[sl, :].astype(F32))
        m_ref[slot, sl, :] = merged.astype(BF16)

    def norm_body(r, c):
        sl = pl.ds(pl.multiple_of(r * norm_rows, norm_rows), norm_rows)
        y = h_ref[sl, :]
        ms = jnp.mean(y * y, axis=-1, keepdims=True)
        h_ref[sl, :] = x_ref[sl, :] + (y * lax.rsqrt(ms + EPS)) * gpost_ref[...]
        return c

    lax.fori_loop(0, tt // norm_rows, norm_body, 0)


def _merge(o_f, o_b, P, out_b, x2d, gla_norm, g_post, w_out, *, tt=512):
    T = x2d.shape[0]
    n_tiles = T // tt
    cur = lambda c: pl.BlockSpec((tt, D_MODEL), lambda s: (jnp.minimum(s, n_tiles - 1), c))
    lag = pl.BlockSpec((tt, D_MODEL), lambda s: (jnp.maximum(s - 1, 0), 0))
    vec = pl.BlockSpec((1, D_MODEL), lambda s: (0, 0))
    return pl.pallas_call(
        functools.partial(_merge_kernel, rows=64, norm_rows=64),
        grid=(n_tiles + 1,),
        in_specs=[
            cur(0), cur(0),
            cur(OFF_G // D_MODEL), cur(OFF_GA // D_MODEL), cur(OFF_GB // D_MODEL),
            cur(0), lag, vec, vec,
            pl.BlockSpec((D_MODEL, D_MODEL), lambda s: (0, 0), pipeline_mode=pl.Buffered(1)),
        ],
        out_specs=lag,
        out_shape=jax.ShapeDtypeStruct((T, D_MODEL), F32),
        scratch_shapes=[pltpu.VMEM((2, tt, D_MODEL), BF16)],
        compiler_params=_cparams(("arbitrary",), 56),
        name="merge_outproj",
    )(o_f, o_b, P, P, P, out_b, x2d, gla_norm, g_post, w_out)


def _ffn_kernel(h_ref, gpre_ref, gpost_ref, w1_ref, w2_ref, y_ref, u_ref, *, rows):
    acc_ref = y_ref
    f = pl.program_id(1)
    tm = h_ref.shape[0]

    @pl.when(f == 0)
    def _():
        def body(r, c):
            sl = pl.ds(pl.multiple_of(r * rows, rows), rows)
            x = h_ref[sl, :]
            ms = jnp.mean(x * x, axis=-1, keepdims=True)
            u_ref[sl, :] = ((x * lax.rsqrt(ms + EPS)) * gpre_ref[...]).astype(BF16)
            return c
        lax.fori_loop(0, tm // rows, body, 0)
        acc_ref[...] = jnp.zeros_like(acc_ref)

    a = jnp.dot(u_ref[...], w1_ref[...], preferred_element_type=F32)
    a = jnp.maximum(a, 0.0)
    a = (a * a).astype(BF16)
    acc_ref[...] += jnp.dot(a, w2_ref[...], preferred_element_type=F32)

    @pl.when(f == pl.num_programs(1) - 1)
    def _():
        def body(r, c):
            sl = pl.ds(pl.multiple_of(r * rows, rows), rows)
            y = acc_ref[sl, :]
            ms = jnp.mean(y * y, axis=-1, keepdims=True)
            y_ref[sl, :] = h_ref[sl, :] + (y * lax.rsqrt(ms + EPS)) * gpost_ref[...]
            return c
        lax.fori_loop(0, tm // rows, body, 0)


def _ffn(h2d, g_pre, g_post, w1, w2, *, tm=1024, tf=1024):
    T = h2d.shape[0]
    vec = pl.BlockSpec((1, D_MODEL), lambda i, f: (0, 0))
    return pl.pallas_call(
        functools.partial(_ffn_kernel, rows=128),
        grid=(T // tm, D_FF // tf),
        in_specs=[
            pl.BlockSpec((tm, D_MODEL), lambda i, f: (i, 0)),
            vec, vec,
            pl.BlockSpec((D_MODEL, tf), lambda i, f: (0, f)),
            pl.BlockSpec((tf, D_MODEL), lambda i, f: (f, 0)),
        ],
        out_specs=pl.BlockSpec((tm, D_MODEL), lambda i, f: (i, 0)),
        out_shape=jax.ShapeDtypeStruct((T, D_MODEL), F32),
        scratch_shapes=[pltpu.VMEM((tm, D_MODEL), BF16)],
        compiler_params=_cparams(("parallel", "arbitrary"), 58),
        name="ffn",
    )(h2d, g_pre, g_post, w1, w2)


def _encoder_layer(x, wts):
    B, S, _ = x.shape
    x2d = x.reshape(B * S, D_MODEL)
    P, Z = _inproj(x2d, wts["g_mix_pre"], wts["w_main"], wts["w_z"])
    o_f, o_b = _gla(P, Z, wts["wa_f"], wts["ba_f"], wts["wa_b"], wts["ba_b"], B, S)
    out_b = _conv_module(P, wts["conv_w"], wts["conv_b"], wts["ln_g"], wts["ln_b"], wts["w_pw2"], B, S)
    h = _merge(o_f, o_b, P, out_b, x2d, wts["gla_norm"], wts["g_mix_post"], wts["w_out"])
    y = _ffn(h, wts["g_ffn_pre"], wts["g_ffn_post"], wts["w_ff1"], wts["w_ff2"])
    return y.reshape(B, S, D_MODEL)


def kernel(x_prompt, x_sample, norm_mix_pre, norm_mix_post, norm_ffn_pre, norm_ffn_post, w_in, w_a_fwd, b_a_fwd, w_a_bwd, b_a_bwd, gla_norm, conv_w, conv_b, conv_ln_g, conv_ln_b, w_pw2, w_out, w_ff1, w_ff2):
    z0 = 2 * D_K + 2 * D_V
    z1 = z0 + 2 * GATE_RANK
    row = lambda v: v.reshape(1, -1).astype(F32)
    w_z = jnp.zeros((D_MODEL, Z_COLS), F32).at[:, :2 * GATE_RANK].set(w_in[:, z0:z1])
    wa_f = jnp.zeros((Z_COLS, D_K), F32).at[:GATE_RANK].set(w_a_fwd)
    wa_b = jnp.zeros((Z_COLS, D_K), F32).at[GATE_RANK:2 * GATE_RANK].set(w_a_bwd)
    wts = {
        "g_mix_pre": row(norm_mix_pre), "g_mix_post": row(norm_mix_post),
        "g_ffn_pre": row(norm_ffn_pre), "g_ffn_post": row(norm_ffn_post),
        "w_main": jnp.concatenate([w_in[:, :z0], w_in[:, z1:]], axis=1).astype(BF16),
        "w_z": w_z.astype(BF16),
        "wa_f": wa_f.astype(BF16), "ba_f": row(b_a_fwd),
        "wa_b": wa_b.astype(BF16), "ba_b": row(b_a_bwd),
        "gla_norm": row(gla_norm),
        "conv_w": conv_w, "conv_b": row(conv_b),
        "ln_g": row(conv_ln_g), "ln_b": row(conv_ln_b),
        "w_pw2": w_pw2.astype(BF16), "w_out": w_out.astype(BF16),
        "w_ff1": w_ff1.astype(BF16), "w_ff2": w_ff2.astype(BF16),
    }
    return (_encoder_layer(x_prompt, wts), _encoder_layer(x_sample, wts))
```

```python
import functools

import jax
import jax.numpy as jnp
from jax import lax
from jax.experimental import pallas as pl
from jax.experimental.pallas import tpu as pltpu

F32 = jnp.float32
BF16 = jnp.bfloat16

D_MODEL = 2048
N_HEADS = 4
D_K = D_MODEL // 2
D_V = D_MODEL
HEAD_DK = D_K // N_HEADS
HEAD_DV = D_V // N_HEADS
GATE_RANK = 16
GATE_TAU = 16.0
CHUNK = 64
CONV_WIDTH = 31
CONV_PAD = (CONV_WIDTH - 1) // 2
D_FF = 4 * D_MODEL
EPS = 1e-6

P_COLS = 7 * D_MODEL
OFF_G = 2 * D_MODEL
OFF_PWA = 3 * D_MODEL
OFF_PWB = 4 * D_MODEL
OFF_GA = 5 * D_MODEL
OFF_GB = 6 * D_MODEL
Z_COLS = 128

HALO = 16
GLA_BLOCK = 2 * CHUNK

_NT = (((1,), (1,)), ((), ()))
_TN = (((0,), (0,)), ((), ()))

_MIB = 1024 * 1024


def _cparams(sem, vmem_mib):
    return pltpu.CompilerParams(dimension_semantics=sem, vmem_limit_bytes=vmem_mib * _MIB)


def _sigmoid(x):
    return 1.0 / (1.0 + jnp.exp(-x))


def _log_sigmoid(x):
    return jnp.minimum(x, 0.0) - jnp.log(1.0 + jnp.exp(-jnp.abs(x)))


def _inproj_kernel(x_ref, g_ref, w_ref, wz_ref, p_ref, z_ref, u_ref, *, rows):
    @pl.when(pl.program_id(1) == 0)
    def _():
        def body(r, c):
            sl = pl.ds(pl.multiple_of(r * rows, rows), rows)
            x = x_ref[sl, :]
            ms = jnp.mean(x * x, axis=-1, keepdims=True)
            u = (x * lax.rsqrt(ms + EPS)) * g_ref[...]
            u_ref[sl, :] = u.astype(BF16)
            return c
        lax.fori_loop(0, x_ref.shape[0] // rows, body, 0)
        z_ref[...] = jnp.dot(u_ref[...], wz_ref[...], preferred_element_type=F32)

    p_ref[...] = lax.dot_general(u_ref[...], w_ref[...], _NT, preferred_element_type=F32).astype(p_ref.dtype)


def _inproj(x2d, g_pre, w_main, w_z, *, tm=1024, tn=2048):
    T = x2d.shape[0]
    n_a = (2 * D_K + 2 * D_V) // tn
    grid = (T // tm, P_COLS // tn)
    return pl.pallas_call(
        functools.partial(_inproj_kernel, rows=256),
        grid=grid,
        in_specs=[
            pl.BlockSpec((tm, D_MODEL), lambda i, j: (i, 0)),
            pl.BlockSpec((1, D_MODEL), lambda i, j: (0, 0)),
            pl.BlockSpec((pl.Element(tn), pl.Element(D_MODEL)),
                         lambda i, j: (pl.multiple_of(j * tn + jnp.where(j >= n_a, 2 * GATE_RANK, 0),
                                                      2 * GATE_RANK), 0)),
            pl.BlockSpec((D_MODEL, Z_COLS), lambda i, j: (0, 0)),
        ],
        out_specs=[
            pl.BlockSpec((tm, tn), lambda i, j: (i, j)),
            pl.BlockSpec((tm, Z_COLS), lambda i, j: (i, 0)),
        ],
        out_shape=[
            jax.ShapeDtypeStruct((T, P_COLS), BF16),
            jax.ShapeDtypeStruct((T, Z_COLS), F32),
        ],
        scratch_shapes=[pltpu.VMEM((tm, D_MODEL), BF16)],
        compiler_params=_cparams(("parallel", "arbitrary"), 52),
        name="inproj",
    )(x2d, g_pre, w_main, w_z)


def _gla_kernel(qf_ref, kf_ref, vf_ref, zf_ref, qb_ref, kb_ref, vb_ref, zb_ref,
                waf_ref, baf_ref, wab_ref, bab_ref, of_ref, ob_ref,
                st_ref, la_ref, b_ref, qk_ref, x_ref, d_ref, p_ref, oi_ref):
    @pl.when(pl.program_id(1) == 0)
    def _():
        st_ref[...] = jnp.zeros_like(st_ref)

    def log_decay(z_ref, w_ref, bias_ref):
        pre = jnp.dot(z_ref[...].astype(BF16), w_ref[...], preferred_element_type=F32) + bias_ref[...]
        return _log_sigmoid(pre) / GATE_TAU

    la_ref[0] = log_decay(zf_ref, waf_ref, baf_ref)
    la_ref[1] = log_decay(zb_ref, wab_ref, bab_ref)

    C = CHUNK
    row3 = lax.broadcasted_iota(jnp.int32, (C, 3 * C), 0)
    col3 = lax.broadcasted_iota(jnp.int32, (C, 3 * C), 1) & (C - 1)
    row = lax.broadcasted_iota(jnp.int32, (C, C), 0)
    col = lax.broadcasted_iota(jnp.int32, (C, C), 1)
    dirs = (
        (qf_ref, kf_ref, vf_ref, of_ref, (col3 <= row3).astype(BF16), col <= row, C // 2, C - 1, 0),
        (qb_ref, kb_ref, vb_ref, ob_ref, (col3 >= row3).astype(BF16), col >= row, C - 1 - C // 2, 0, 1),
    )
    n_blocks = qf_ref.shape[0] // GLA_BLOCK
    chains = [(dr, hd) for dr in range(2) for hd in range(N_HEADS)]
    klanes = lambda hd: slice(hd * HEAD_DK, (hd + 1) * HEAD_DK)
    vlanes = lambda hd: slice(hd * HEAD_DV, (hd + 1) * HEAD_DV)
    sub = lambda sc: slice(sc * C, (sc + 1) * C)
    zeros_cc = jnp.zeros((C, C), F32)

    def body(c, carry):
        rows = (pl.ds(pl.multiple_of(c * GLA_BLOCK, GLA_BLOCK), GLA_BLOCK),
                pl.ds(pl.multiple_of((n_blocks - 1 - c) * GLA_BLOCK, GLA_BLOCK), GLA_BLOCK))

        for dr in range(2):
            tri3, last_idx, first = dirs[dr][4], dirs[dr][7], dirs[dr][8]
            la = la_ref[dr, rows[dr], :]
            h1 = la.astype(BF16)
            r1 = la - h1.astype(F32)
            h2 = r1.astype(BF16)
            h3 = (r1 - h2.astype(F32)).astype(BF16)
            total = []
            for sc in range(2):
                b = jnp.dot(tri3, jnp.concatenate([h1[sub(sc)], h2[sub(sc)], h3[sub(sc)]], axis=0),
                            preferred_element_type=F32)
                b_ref[dr, sub(sc), :] = b
                total.append(b[last_idx:last_idx + 1, :])
            d_ref[dr, 0:1, :] = jnp.exp(total[0] + total[1])
            d_ref[dr, 1:2, :] = jnp.exp(total[first])
            d_ref[dr, 2:3, :] = jnp.exp(total[1 - first])

        for dr, hd in chains:
            q_ref, k_ref, ref_idx, last_idx, first = (dirs[dr][0], dirs[dr][1], dirs[dr][6],
                                                      dirs[dr][7], dirs[dr][8])
            kl = klanes(hd)
            b = b_ref[dr, :, kl]

            def per_chunk(idx):
                return jnp.concatenate([jnp.broadcast_to(b[sc * C + idx:sc * C + idx + 1, :], (C, HEAD_DK))
                                        for sc in range(2)], axis=0)

            b_mid, b_last = per_chunk(ref_idx), per_chunk(last_idx)
            q = q_ref[rows[dr], kl].astype(F32) * (HEAD_DK ** -0.5)
            k = k_ref[rows[dr], kl].astype(F32)
            q_loc = q * jnp.exp(b)
            k_loc = k * jnp.exp(b_last - b)
            qk_ref[dr, 1, :, kl] = (q * jnp.exp(b - b_mid)).astype(BF16)
            qk_ref[dr, 2, :, kl] = (k * jnp.exp(b_mid - b)).astype(BF16)
            early, late = sub(first), sub(1 - first)
            qk_ref[dr, 0, early, kl] = q_loc[early].astype(BF16)
            qk_ref[dr, 0, late, kl] = (q_loc[late] * d_ref[dr, 1:2, kl]).astype(BF16)
            qk_ref[dr, 3, early, kl] = (k_loc[early] * d_ref[dr, 2:3, kl]).astype(BF16)
            qk_ref[dr, 3, late, kl] = k_loc[late].astype(BF16)
            x_ref[dr, 0, :, kl] = q_loc[late].astype(BF16)
            x_ref[dr, 1, :, kl] = k_loc[early].astype(BF16)

        for dr, hd in chains:
            mask, first = dirs[dr][5], dirs[dr][8]
            kl = klanes(hd)
            diag = []
            for sc in range(2):
                s = lax.dot_general(qk_ref[dr, 1, sub(sc), kl], qk_ref[dr, 2, sub(sc), kl], _NT,
                                    preferred_element_type=F32)
                diag.append(jnp.where(mask, s, 0.0))
            cross = lax.dot_general(x_ref[dr, 0, :, kl], x_ref[dr, 1, :, kl], _NT, preferred_element_type=F32)
            if first == 0:
                top = jnp.concatenate([diag[0], zeros_cc], axis=1)
                bottom = jnp.concatenate([cross, diag[1]], axis=1)
            else:
                top = jnp.concatenate([diag[0], cross], axis=1)
                bottom = jnp.concatenate([zeros_cc, diag[1]], axis=1)
            p_ref[dr * N_HEADS + hd] = jnp.concatenate([top, bottom], axis=0).astype(BF16)

        for dr, hd in chains:
            v_ref = dirs[dr][2]
            ch = dr * N_HEADS + hd
            st = st_ref[ch]
            oi_ref[ch] = lax.dot_general(qk_ref[dr, 0, :, klanes(hd)], st.astype(BF16), _NT,
                                         preferred_element_type=F32)
            upd = lax.dot_general(v_ref[rows[dr], vlanes(hd)], qk_ref[dr, 3, :, klanes(hd)], _TN,
                                  preferred_element_type=F32)
            st_ref[ch] = d_ref[dr, 0:1, klanes(hd)] * st + upd

        for dr, hd in chains:
            v_ref, o_ref = dirs[dr][2], dirs[dr][3]
            ch = dr * N_HEADS + hd
            o = oi_ref[ch] + jnp.dot(p_ref[ch], v_ref[rows[dr], vlanes(hd)], preferred_element_type=F32)
            o_ref[rows[dr], vlanes(hd)] = o.astype(o_ref.dtype)
        return carry

    lax.fori_loop(0, n_blocks, body, 0)


def _gla(P, Z, wa_f, ba_f, wa_b, ba_b, B, S, *, ts=512):
    T = B * S
    nb = S // ts
    fwd = lambda b, i: b * nb + i
    bwd = lambda b, i: b * nb + (nb - 1 - i)

    def stream_specs(rowmap):
        return [
            pl.BlockSpec((ts, D_K), lambda b, i: (rowmap(b, i), 0)),
            pl.BlockSpec((ts, D_K), lambda b, i: (rowmap(b, i), 1)),
            pl.BlockSpec((ts, D_V), lambda b, i: (rowmap(b, i), 1)),
            pl.BlockSpec((ts, Z_COLS), lambda b, i: (rowmap(b, i), 0)),
        ]

    w_spec = pl.BlockSpec((Z_COLS, D_K), lambda b, i: (0, 0))
    b_spec = pl.BlockSpec((1, D_K), lambda b, i: (0, 0))
    return pl.pallas_call(
        _gla_kernel,
        grid=(B, nb),
        in_specs=stream_specs(fwd) + stream_specs(bwd) + [w_spec, b_spec, w_spec, b_spec],
        out_specs=[
            pl.BlockSpec((ts, D_V), lambda b, i: (fwd(b, i), 0)),
            pl.BlockSpec((ts, D_V), lambda b, i: (bwd(b, i), 0)),
        ],
        out_shape=[jax.ShapeDtypeStruct((T, D_V), BF16)] * 2,
        scratch_shapes=[
            pltpu.VMEM((2 * N_HEADS, HEAD_DV, HEAD_DK), F32),
            pltpu.VMEM((2, ts, D_K), F32),
            pltpu.VMEM((2, GLA_BLOCK, D_K), F32),
            pltpu.VMEM((2, 4, GLA_BLOCK, D_K), BF16),
            pltpu.VMEM((2, 2, CHUNK, D_K), BF16),
            pltpu.VMEM((2, 8, D_K), F32),
            pltpu.VMEM((2 * N_HEADS, GLA_BLOCK, GLA_BLOCK), BF16),
            pltpu.VMEM((2 * N_HEADS, GLA_BLOCK, HEAD_DV), F32),
        ],
        compiler_params=_cparams(("parallel", "arbitrary"), 56),
        name="gla",
    )(P, P, P, Z, P, P, P, Z, wa_f, ba_f, wa_b, ba_b)


def _conv_kernel(pa_ref, pb_ref, pa_prev, pb_prev, pa_next, pb_next,
                 cw_ref, cb_ref, lg_ref, lb_ref, w2_ref, out_ref,
                 glu_ref, dw_ref, act_ref, *, rb, lb, nt):
    s = pl.program_id(0)
    tile = jnp.minimum(s, pl.num_programs(0) - 2)
    i = tile % nt
    slot = s % 2
    tt = pa_ref.shape[0]

    @pl.when(s == 0)
    def _():
        act_ref[1] = jnp.zeros(act_ref.shape[1:], act_ref.dtype)

    def glu(a, b):
        return a.astype(F32) * _sigmoid(b.astype(F32))

    n_win = tt // rb
    body_rows = rb - HALO
    glu_ref[0, 0:HALO, :] = jnp.where(i > 0, glu(pa_prev[...], pb_prev[...]), 0.0)
    glu_ref[0, HALO:rb, :] = glu(pa_ref[0:body_rows, :], pb_ref[0:body_rows, :])
    glu_ref[n_win - 1, rb:rb + HALO, :] = glu(pa_ref[tt - HALO:tt, :], pb_ref[tt - HALO:tt, :])
    glu_ref[n_win - 1, rb + HALO:rb + 2 * HALO, :] = jnp.where(
        i < nt - 1, glu(pa_next[...], pb_next[...]), 0.0)

    def fill_body(m, c):
        src = pl.ds(pl.multiple_of(m * rb - HALO, HALO), rb)
        val = glu(pa_ref[src, :], pb_ref[src, :])
        glu_ref[m, 0:rb, :] = val
        glu_ref[m - 1, rb:rb + 2 * HALO, :] = val[0:2 * HALO]
        return c

    lax.fori_loop(1, n_win, fill_body, 0)

    sub = 8
    shift = HALO - CONV_PAD
    n_col = D_MODEL // n_win

    for m in range(n_win):
        cols = slice(m * n_col, (m + 1) * n_col)
        out_ref[:, cols] = jnp.dot(act_ref[1 - slot], w2_ref[:, cols],
                                   preferred_element_type=F32).astype(out_ref.dtype)
        out_rows = slice(m * rb, (m + 1) * rb)
        for l0 in range(0, D_MODEL, lb):
            lanes = slice(l0, l0 + lb)
            acc = None
            for ph in range(sub):
                part = None
                for q in range((CONV_WIDTH + shift) // sub + 1):
                    j = sub * q + ph - shift
                    if 0 <= j < CONV_WIDTH:
                        term = cw_ref[j:j + 1, lanes] * glu_ref[m, pl.ds(sub * q, rb + sub), lanes]
                        part = term if part is None else part + term
                shifted = part[ph:ph + rb]
                acc = shifted if acc is None else acc + shifted
            dw_ref[out_rows, lanes] = acc + cb_ref[:, lanes]

    ln_rows = 128

    def ln_body(r, c):
        sl = pl.ds(pl.multiple_of(r * ln_rows, ln_rows), ln_rows)
        x = dw_ref[sl, :]
        mu = jnp.mean(x, axis=-1, keepdims=True)
        xc = x - mu
        var = jnp.mean(xc * xc, axis=-1, keepdims=True)
        y = xc * lax.rsqrt(var + EPS) * lg_ref[...] + lb_ref[...]
        act_ref[slot, sl, :] = (y * _sigmoid(y)).astype(BF16)
        return c

    lax.fori_loop(0, tt // ln_rows, ln_body, 0)


def _conv_module(P, conv_w, conv_b, ln_g, ln_b, w_pw2, B, S, *, tt=512):
    T = B * S
    nt = S // tt
    n_tiles = B * nt
    hb = tt // HALO
    n_halo_blocks = T // HALO
    ca, cb = OFF_PWA // D_MODEL, OFF_PWB // D_MODEL
    tile = lambda s: jnp.minimum(s, n_tiles - 1)
    prev = lambda s: jnp.maximum(tile(s) * hb - 1, 0)
    nxt = lambda s: jnp.minimum((tile(s) + 1) * hb, n_halo_blocks - 1)
    vec = lambda n: pl.BlockSpec((n, D_MODEL), lambda s: (0, 0))
    rb = 64
    return pl.pallas_call(
        functools.partial(_conv_kernel, rb=rb, lb=128, nt=nt),
        grid=(n_tiles + 1,),
        in_specs=[
            pl.BlockSpec((tt, D_MODEL), lambda s: (tile(s), ca)),
            pl.BlockSpec((tt, D_MODEL), lambda s: (tile(s), cb)),
            pl.BlockSpec((HALO, D_MODEL), lambda s: (prev(s), ca)),
            pl.BlockSpec((HALO, D_MODEL), lambda s: (prev(s), cb)),
            pl.BlockSpec((HALO, D_MODEL), lambda s: (nxt(s), ca)),
            pl.BlockSpec((HALO, D_MODEL), lambda s: (nxt(s), cb)),
            vec(CONV_WIDTH), vec(1), vec(1), vec(1),
            pl.BlockSpec((D_MODEL, D_MODEL), lambda s: (0, 0), pipeline_mode=pl.Buffered(1)),
        ],
        out_specs=pl.BlockSpec((tt, D_MODEL), lambda s: (jnp.maximum(s - 1, 0), 0)),
        out_shape=jax.ShapeDtypeStruct((T, D_MODEL), BF16),
        scratch_shapes=[
            pltpu.VMEM((tt // rb, rb + 2 * HALO, D_MODEL), F32),
            pltpu.VMEM((tt, D_MODEL), F32),
            pltpu.VMEM((2, tt, D_MODEL), BF16),
        ],
        compiler_params=_cparams(("arbitrary",), 52),
        name="conv_module",
    )(P, P, P, P, P, P, conv_w, conv_b, ln_g, ln_b, w_pw2)


def _merge_kernel(of_ref, ob_ref, g_ref, ga_ref, gb_ref, cb_ref, x_ref, gn_ref, gpost_ref, wo_ref,
                  h_ref, m_ref, *, rows, norm_rows):
    s = pl.program_id(0)
    slot = s % 2
    tt = x_ref.shape[0]
    n_blk = tt // rows
    n_col = D_MODEL // n_blk

    @pl.when(s == 0)
    def _():
        m_ref[1] = jnp.zeros(m_ref.shape[1:], m_ref.dtype)

    for r in range(n_blk):
        cols = slice(r * n_col, (r + 1) * n_col)
        h_ref[:, cols] = jnp.dot(m_ref[1 - slot], wo_ref[:, cols], preferred_element_type=F32)
        sl = slice(r * rows, (r + 1) * rows)
        o = of_ref[sl, :].astype(F32) + ob_ref[sl, :].astype(F32)
        heads = []
        for hd in range(N_HEADS):
            oh = o[:, hd * HEAD_DV:(hd + 1) * HEAD_DV]
            ms = jnp.mean(oh * oh, axis=-1, keepdims=True)
            heads.append(oh * lax.rsqrt(ms + EPS))
        on = jnp.concatenate(heads, axis=-1) * gn_ref[...]
        g = g_ref[sl, :].astype(F32)
        out_a = on * (g * _sigmoid(g))
        merged = (_sigmoid(ga_ref[sl, :].astype(F32)) * out_a
                  + _sigmoid(gb_ref[sl, :].astype(F32)) * cb_ref[sl, :].astype(F32))
        m_ref[slot, sl, :] = merged.astype(BF16)

    def norm_body(r, c):
        sl = pl.ds(pl.multiple_of(r * norm_rows, norm_rows), norm_rows)
        y = h_ref[sl, :]
        ms = jnp.mean(y * y, axis=-1, keepdims=True)
        h_ref[sl, :] = x_ref[sl, :] + (y * lax.rsqrt(ms + EPS)) * gpost_ref[...]
        return c

    lax.fori_loop(0, tt // norm_rows, norm_body, 0)


def _merge(o_f, o_b, P, out_b, x2d, gla_norm, g_post, w_out, *, tt=512):
    T = x2d.shape[0]
    n_tiles = T // tt
    cur = lambda c: pl.BlockSpec((tt, D_MODEL), lambda s: (jnp.minimum(s, n_tiles - 1), c))
    lag = pl.BlockSpec((tt, D_MODEL), lambda s: (jnp.maximum(s - 1, 0), 0))
    vec = pl.BlockSpec((1, D_MODEL), lambda s: (0, 0))
    return pl.pallas_call(
        functools.partial(_merge_kernel, rows=64, norm_rows=64),
        grid=(n_tiles + 1,),
        in_specs=[
            cur(0), cur(0),
            cur(OFF_G // D_MODEL), cur(OFF_GA // D_MODEL), cur(OFF_GB // D_MODEL),
            cur(0), lag, vec, vec,
            pl.BlockSpec((D_MODEL, D_MODEL), lambda s: (0, 0), pipeline_mode=pl.Buffered(1)),
        ],
        out_specs=lag,
        out_shape=jax.ShapeDtypeStruct((T, D_MODEL), F32),
        scratch_shapes=[pltpu.VMEM((2, tt, D_MODEL), BF16)],
        compiler_params=_cparams(("arbitrary",), 56),
        name="merge_outproj",
    )(o_f, o_b, P, P, P, out_b, x2d, gla_norm, g_post, w_out)


def _ffn_kernel(h_ref, gpre_ref, gpost_ref, w1_ref, w2_ref, y_ref, u_ref, *, rows):
    acc_ref = y_ref
    f = pl.program_id(1)
    tm = h_ref.shape[0]

    @pl.when(f == 0)
    def _():
        def body(r, c):
            sl = pl.ds(pl.multiple_of(r * rows, rows), rows)
            x = h_ref[sl, :]
            ms = jnp.mean(x * x, axis=-1, keepdims=True)
            u_ref[sl, :] = ((x * lax.rsqrt(ms + EPS)) * gpre_ref[...]).astype(BF16)
            return c
        lax.fori_loop(0, tm // rows, body, 0)
        acc_ref[...] = jnp.zeros_like(acc_ref)

    a = jnp.dot(u_ref[...], w1_ref[...], preferred_element_type=F32)
    a = jnp.maximum(a, 0.0)
    a = (a * a).astype(BF16)
    acc_ref[...] += jnp.dot(a, w2_ref[...], preferred_element_type=F32)

    @pl.when(f == pl.num_programs(1) - 1)
    def _():
        def body(r, c):
            sl = pl.ds(pl.multiple_of(r * rows, rows), rows)
            y = acc_ref[sl, :]
            ms = jnp.mean(y * y, axis=-1, keepdims=True)
            y_ref[sl, :] = h_ref[sl, :] + (y * lax.rsqrt(ms + EPS)) * gpost_ref[...]
            return c
        lax.fori_loop(0, tm // rows, body, 0)


def _ffn(h2d, g_pre, g_post, w1, w2, *, tm=1024, tf=1024):
    T = h2d.shape[0]
    vec = pl.BlockSpec((1, D_MODEL), lambda i, f: (0, 0))
    return pl.pallas_call(
        functools.partial(_ffn_kernel, rows=128),
        grid=(T // tm, D_FF // tf),
        in_specs=[
            pl.BlockSpec((tm, D_MODEL), lambda i, f: (i, 0)),
            vec, vec,
            pl.BlockSpec((D_MODEL, tf), lambda i, f: (0, f)),
            pl.BlockSpec((tf, D_MODEL), lambda i, f: (f, 0)),
        ],
        out_specs=pl.BlockSpec((tm, D_MODEL), lambda i, f: (i, 0)),
        out_shape=jax.ShapeDtypeStruct((T, D_MODEL), F32),
        scratch_shapes=[pltpu.VMEM((tm, D_MODEL), BF16)],
        compiler_params=_cparams(("parallel", "arbitrary"), 58),
        name="ffn",
    )(h2d, g_pre, g_post, w1, w2)


def _encoder_layer(x, wts):
    B, S, _ = x.shape
    x2d = x.reshape(B * S, D_MODEL)
    P, Z = _inproj(x2d, wts["g_mix_pre"], wts["w_main"], wts["w_z"])
    o_f, o_b = _gla(P, Z, wts["wa_f"], wts["ba_f"], wts["wa_b"], wts["ba_b"], B, S)
    out_b = _conv_module(P, wts["conv_w"], wts["conv_b"], wts["ln_g"], wts["ln_b"], wts["w_pw2"], B, S)
    h = _merge(o_f, o_b, P, out_b, x2d, wts["gla_norm"], wts["g_mix_post"], wts["w_out"])
    y = _ffn(h, wts["g_ffn_pre"], wts["g_ffn_post"], wts["w_ff1"], wts["w_ff2"])
    return y.reshape(B, S, D_MODEL)


def kernel(x_prompt, x_sample, norm_mix_pre, norm_mix_post, norm_ffn_pre, norm_ffn_post, w_in, w_a_fwd, b_a_fwd, w_a_bwd, b_a_bwd, gla_norm, conv_w, conv_b, conv_ln_g, conv_ln_b, w_pw2, w_out, w_ff1, w_ff2):
    z0 = 2 * D_K + 2 * D_V
    z1 = z0 + 2 * GATE_RANK
    row = lambda v: v.reshape(1, -1).astype(F32)
    w_z = jnp.zeros((D_MODEL, Z_COLS), F32).at[:, :2 * GATE_RANK].set(w_in[:, z0:z1])
    wa_f = jnp.zeros((Z_COLS, D_K), F32).at[:GATE_RANK].set(w_a_fwd)
    wa_b = jnp.zeros((Z_COLS, D_K), F32).at[GATE_RANK:2 * GATE_RANK].set(w_a_bwd)
    wts = {
        "g_mix_pre": row(norm_mix_pre), "g_mix_post": row(norm_mix_post),
        "g_ffn_pre": row(norm_ffn_pre), "g_ffn_post": row(norm_ffn_post),
        "w_main": w_in.T.astype(BF16),
        "w_z": w_z.astype(BF16),
        "wa_f": wa_f.astype(BF16), "ba_f": row(b_a_fwd),
        "wa_b": wa_b.astype(BF16), "ba_b": row(b_a_bwd),
        "gla_norm": row(gla_norm),
        "conv_w": conv_w, "conv_b": row(conv_b),
        "ln_g": row(conv_ln_g), "ln_b": row(conv_ln_b),
        "w_pw2": w_pw2.astype(BF16), "w_out": w_out.astype(BF16),
        "w_ff1": w_ff1.astype(BF16), "w_ff2": w_ff2.astype(BF16),
    }
    return (_encoder_layer(x_prompt, wts), _encoder_layer(x_sample, wts))
```

```python
import functools

import jax
import jax.numpy as jnp
from jax import lax
from jax.experimental import pallas as pl
from jax.experimental.pallas import tpu as pltpu

F32 = jnp.float32
BF16 = jnp.bfloat16

D_MODEL = 2048
N_HEADS = 4
D_K = D_MODEL // 2
D_V = D_MODEL
HEAD_DK = D_K // N_HEADS
HEAD_DV = D_V // N_HEADS
GATE_RANK = 16
GATE_TAU = 16.0
CHUNK = 64
CONV_WIDTH = 31
CONV_PAD = (CONV_WIDTH - 1) // 2
D_FF = 4 * D_MODEL
EPS = 1e-6

P_COLS = 7 * D_MODEL
OFF_G = 2 * D_MODEL
OFF_PWA = 3 * D_MODEL
OFF_PWB = 4 * D_MODEL
OFF_GA = 5 * D_MODEL
OFF_GB = 6 * D_MODEL
Z_COLS = 128

HALO = 16
GLA_BLOCK = 2 * CHUNK

_NT = (((1,), (1,)), ((), ()))
_TN = (((0,), (0,)), ((), ()))

_MIB = 1024 * 1024


def _cparams(sem, vmem_mib):
    return pltpu.CompilerParams(dimension_semantics=sem, vmem_limit_bytes=vmem_mib * _MIB)


def _sigmoid(x):
    return 1.0 / (1.0 + jnp.exp(-x))


def _log_sigmoid(x):
    return jnp.minimum(x, 0.0) - jnp.log(1.0 + jnp.exp(-jnp.abs(x)))


def _inproj_kernel(x_ref, g_ref, w_ref, wz_ref, p_ref, z_ref, u_ref, *, rows):
    @pl.when(pl.program_id(1) == 0)
    def _():
        def body(r, c):
            sl = pl.ds(pl.multiple_of(r * rows, rows), rows)
            x = x_ref[sl, :]
            ms = jnp.mean(x * x, axis=-1, keepdims=True)
            u = (x * lax.rsqrt(ms + EPS)) * g_ref[...]
            u_ref[sl, :] = u.astype(BF16)
            return c
        lax.fori_loop(0, x_ref.shape[0] // rows, body, 0)
        z_ref[...] = jnp.dot(u_ref[...], wz_ref[...], preferred_element_type=F32)

    p_ref[...] = lax.dot_general(u_ref[...], w_ref[...], _NT, preferred_element_type=F32).astype(p_ref.dtype)


def _inproj(x2d, g_pre, w_main, w_z, *, tm=1024, tn=2048):
    T = x2d.shape[0]
    n_a = (2 * D_K + 2 * D_V) // tn
    grid = (T // tm, P_COLS // tn)
    return pl.pallas_call(
        functools.partial(_inproj_kernel, rows=256),
        grid=grid,
        in_specs=[
            pl.BlockSpec((tm, D_MODEL), lambda i, j: (i, 0)),
            pl.BlockSpec((1, D_MODEL), lambda i, j: (0, 0)),
            pl.BlockSpec((pl.Element(tn), pl.Element(D_MODEL)),
                         lambda i, j: (pl.multiple_of(j * tn + jnp.where(j >= n_a, 2 * GATE_RANK, 0),
                                                      2 * GATE_RANK), 0)),
            pl.BlockSpec((D_MODEL, Z_COLS), lambda i, j: (0, 0)),
        ],
        out_specs=[
            pl.BlockSpec((tm, tn), lambda i, j: (i, j)),
            pl.BlockSpec((tm, Z_COLS), lambda i, j: (i, 0)),
        ],
        out_shape=[
            jax.ShapeDtypeStruct((T, P_COLS), BF16),
            jax.ShapeDtypeStruct((T, Z_COLS), F32),
        ],
        scratch_shapes=[pltpu.VMEM((tm, D_MODEL), BF16)],
        compiler_params=_cparams(("parallel", "arbitrary"), 52),
        name="inproj",
    )(x2d, g_pre, w_main, w_z)


def _gla_kernel(qkf_ref, vf_ref, zf_ref, qkb_ref, vb_ref, zb_ref,
                waf_ref, baf_ref, wab_ref, bab_ref, of_ref, ob_ref,
                st_ref, la_ref, b_ref, qk_ref, x_ref, d_ref, p_ref, oi_ref):
    @pl.when(pl.program_id(1) == 0)
    def _():
        st_ref[...] = jnp.zeros_like(st_ref)

    def log_decay(z_ref, w_ref, bias_ref):
        pre = jnp.dot(z_ref[...].astype(BF16), w_ref[...], preferred_element_type=F32) + bias_ref[...]
        return _log_sigmoid(pre) / GATE_TAU

    la_ref[0] = log_decay(zf_ref, waf_ref, baf_ref)
    la_ref[1] = log_decay(zb_ref, wab_ref, bab_ref)

    C = CHUNK
    row3 = lax.broadcasted_iota(jnp.int32, (C, 3 * C), 0)
    col3 = lax.broadcasted_iota(jnp.int32, (C, 3 * C), 1) & (C - 1)
    row = lax.broadcasted_iota(jnp.int32, (C, C), 0)
    col = lax.broadcasted_iota(jnp.int32, (C, C), 1)
    dirs = (
        (qkf_ref, qkf_ref, vf_ref, of_ref, (col3 <= row3).astype(BF16), col <= row, C // 2, C - 1, 0),
        (qkb_ref, qkb_ref, vb_ref, ob_ref, (col3 >= row3).astype(BF16), col >= row, C - 1 - C // 2, 0, 1),
    )
    n_blocks = qkf_ref.shape[0] // GLA_BLOCK
    chains = [(dr, hd) for dr in range(2) for hd in range(N_HEADS)]
    klanes = lambda hd: slice(hd * HEAD_DK, (hd + 1) * HEAD_DK)
    vlanes = lambda hd: slice(hd * HEAD_DV, (hd + 1) * HEAD_DV)
    sub = lambda sc: slice(sc * C, (sc + 1) * C)
    zeros_cc = jnp.zeros((C, C), F32)

    def body(c, carry):
        rows = (pl.ds(pl.multiple_of(c * GLA_BLOCK, GLA_BLOCK), GLA_BLOCK),
                pl.ds(pl.multiple_of((n_blocks - 1 - c) * GLA_BLOCK, GLA_BLOCK), GLA_BLOCK))

        for dr in range(2):
            tri3, last_idx, first = dirs[dr][4], dirs[dr][7], dirs[dr][8]
            la = la_ref[dr, rows[dr], :]
            h1 = la.astype(BF16)
            r1 = la - h1.astype(F32)
            h2 = r1.astype(BF16)
            h3 = (r1 - h2.astype(F32)).astype(BF16)
            total = []
            for sc in range(2):
                b = jnp.dot(tri3, jnp.concatenate([h1[sub(sc)], h2[sub(sc)], h3[sub(sc)]], axis=0),
                            preferred_element_type=F32)
                b_ref[dr, sub(sc), :] = b
                total.append(b[last_idx:last_idx + 1, :])
            d_ref[dr, 0:1, :] = jnp.exp(total[0] + total[1])
            d_ref[dr, 1:2, :] = jnp.exp(total[first])
            d_ref[dr, 2:3, :] = jnp.exp(total[1 - first])

        for dr, hd in chains:
            q_ref, k_ref, ref_idx, last_idx, first = (dirs[dr][0], dirs[dr][1], dirs[dr][6],
                                                      dirs[dr][7], dirs[dr][8])
            kl = klanes(hd)
            b = b_ref[dr, :, kl]

            def per_chunk(idx):
                return jnp.concatenate([jnp.broadcast_to(b[sc * C + idx:sc * C + idx + 1, :], (C, HEAD_DK))
                                        for sc in range(2)], axis=0)

            b_mid, b_last = per_chunk(ref_idx), per_chunk(last_idx)
            q = q_ref[rows[dr], kl].astype(F32) * (HEAD_DK ** -0.5)
            k = k_ref[rows[dr], slice(D_K + kl.start, D_K + kl.stop)].astype(F32)
            q_loc = q * jnp.exp(b)
            k_loc = k * jnp.exp(b_last - b)
            qk_ref[dr, 1, :, kl] = (q * jnp.exp(b - b_mid)).astype(BF16)
            qk_ref[dr, 2, :, kl] = (k * jnp.exp(b_mid - b)).astype(BF16)
            early, late = sub(first), sub(1 - first)
            qk_ref[dr, 0, early, kl] = q_loc[early].astype(BF16)
            qk_ref[dr, 0, late, kl] = (q_loc[late] * d_ref[dr, 1:2, kl]).astype(BF16)
            qk_ref[dr, 3, early, kl] = (k_loc[early] * d_ref[dr, 2:3, kl]).astype(BF16)
            qk_ref[dr, 3, late, kl] = k_loc[late].astype(BF16)
            x_ref[dr, 0, :, kl] = q_loc[late].astype(BF16)
            x_ref[dr, 1, :, kl] = k_loc[early].astype(BF16)

        for dr, hd in chains:
            mask, first = dirs[dr][5], dirs[dr][8]
            kl = klanes(hd)
            diag = []
            for sc in range(2):
                s = lax.dot_general(qk_ref[dr, 1, sub(sc), kl], qk_ref[dr, 2, sub(sc), kl], _NT,
                                    preferred_element_type=F32)
                diag.append(jnp.where(mask, s, 0.0))
            cross = lax.dot_general(x_ref[dr, 0, :, kl], x_ref[dr, 1, :, kl], _NT, preferred_element_type=F32)
            if first == 0:
                top = jnp.concatenate([diag[0], zeros_cc], axis=1)
                bottom = jnp.concatenate([cross, diag[1]], axis=1)
            else:
                top = jnp.concatenate([diag[0], cross], axis=1)
                bottom = jnp.concatenate([zeros_cc, diag[1]], axis=1)
            p_ref[dr * N_HEADS + hd] = jnp.concatenate([top, bottom], axis=0).astype(BF16)

        for dr, hd in chains:
            v_ref = dirs[dr][2]
            ch = dr * N_HEADS + hd
            st = st_ref[ch]
            oi_ref[ch] = lax.dot_general(qk_ref[dr, 0, :, klanes(hd)], st.astype(BF16), _NT,
                                         preferred_element_type=F32)
            upd = lax.dot_general(v_ref[rows[dr], vlanes(hd)], qk_ref[dr, 3, :, klanes(hd)], _TN,
                                  preferred_element_type=F32)
            st_ref[ch] = d_ref[dr, 0:1, klanes(hd)] * st + upd

        for dr, hd in chains:
            v_ref, o_ref = dirs[dr][2], dirs[dr][3]
            ch = dr * N_HEADS + hd
            o = oi_ref[ch] + jnp.dot(p_ref[ch], v_ref[rows[dr], vlanes(hd)], preferred_element_type=F32)
            o_ref[rows[dr], vlanes(hd)] = o.astype(o_ref.dtype)
        return carry

    lax.fori_loop(0, n_blocks, body, 0)


def _gla(P, Z, wa_f, ba_f, wa_b, ba_b, B, S, *, ts=512):
    T = B * S
    nb = S // ts
    fwd = lambda b, i: b * nb + i
    bwd = lambda b, i: b * nb + (nb - 1 - i)

    def stream_specs(rowmap):
        return [
            pl.BlockSpec((ts, 2 * D_K), lambda b, i: (rowmap(b, i), 0)),
            pl.BlockSpec((ts, D_V), lambda b, i: (rowmap(b, i), 1)),
            pl.BlockSpec((ts, Z_COLS), lambda b, i: (rowmap(b, i), 0)),
        ]

    w_spec = pl.BlockSpec((Z_COLS, D_K), lambda b, i: (0, 0))
    b_spec = pl.BlockSpec((1, D_K), lambda b, i: (0, 0))
    return pl.pallas_call(
        _gla_kernel,
        grid=(B, nb),
        in_specs=stream_specs(fwd) + stream_specs(bwd) + [w_spec, b_spec, w_spec, b_spec],
        out_specs=[
            pl.BlockSpec((ts, D_V), lambda b, i: (fwd(b, i), 0)),
            pl.BlockSpec((ts, D_V), lambda b, i: (bwd(b, i), 0)),
        ],
        out_shape=[jax.ShapeDtypeStruct((T, D_V), BF16)] * 2,
        scratch_shapes=[
            pltpu.VMEM((2 * N_HEADS, HEAD_DV, HEAD_DK), F32),
            pltpu.VMEM((2, ts, D_K), F32),
            pltpu.VMEM((2, GLA_BLOCK, D_K), F32),
            pltpu.VMEM((2, 4, GLA_BLOCK, D_K), BF16),
            pltpu.VMEM((2, 2, CHUNK, D_K), BF16),
            pltpu.VMEM((2, 8, D_K), F32),
            pltpu.VMEM((2 * N_HEADS, GLA_BLOCK, GLA_BLOCK), BF16),
            pltpu.VMEM((2 * N_HEADS, GLA_BLOCK, HEAD_DV), F32),
        ],
        compiler_params=_cparams(("parallel", "arbitrary"), 56),
        name="gla",
    )(P, P, Z, P, P, Z, wa_f, ba_f, wa_b, ba_b)


def _conv_kernel(pa_ref, pb_ref, pa_prev, pb_prev, pa_next, pb_next,
                 cw_ref, cb_ref, lg_ref, lb_ref, w2_ref, out_ref,
                 glu_ref, dw_ref, act_ref, *, rb, lb, nt):
    s = pl.program_id(0)
    tile = jnp.minimum(s, pl.num_programs(0) - 2)
    i = tile % nt
    slot = s % 2
    tt = pa_ref.shape[0]

    @pl.when(s == 0)
    def _():
        act_ref[1] = jnp.zeros(act_ref.shape[1:], act_ref.dtype)

    def glu(a, b):
        return a.astype(F32) * _sigmoid(b.astype(F32))

    n_win = tt // rb
    body_rows = rb - HALO
    glu_ref[0, 0:HALO, :] = jnp.where(i > 0, glu(pa_prev[...], pb_prev[...]), 0.0)
    glu_ref[0, HALO:rb, :] = glu(pa_ref[0:body_rows, :], pb_ref[0:body_rows, :])
    glu_ref[n_win - 1, rb:rb + HALO, :] = glu(pa_ref[tt - HALO:tt, :], pb_ref[tt - HALO:tt, :])
    glu_ref[n_win - 1, rb + HALO:rb + 2 * HALO, :] = jnp.where(
        i < nt - 1, glu(pa_next[...], pb_next[...]), 0.0)

    def fill_body(m, c):
        src = pl.ds(pl.multiple_of(m * rb - HALO, HALO), rb)
        val = glu(pa_ref[src, :], pb_ref[src, :])
        glu_ref[m, 0:rb, :] = val
        glu_ref[m - 1, rb:rb + 2 * HALO, :] = val[0:2 * HALO]
        return c

    lax.fori_loop(1, n_win, fill_body, 0)

    sub = 8
    shift = HALO - CONV_PAD
    n_col = D_MODEL // n_win

    for m in range(n_win):
        cols = slice(m * n_col, (m + 1) * n_col)
        out_ref[:, cols] = jnp.dot(act_ref[1 - slot], w2_ref[:, cols],
                                   preferred_element_type=F32).astype(out_ref.dtype)
        out_rows = slice(m * rb, (m + 1) * rb)
        for l0 in range(0, D_MODEL, lb):
            lanes = slice(l0, l0 + lb)
            acc = None
            for ph in range(sub):
                part = None
                for q in range((CONV_WIDTH + shift) // sub + 1):
                    j = sub * q + ph - shift
                    if 0 <= j < CONV_WIDTH:
                        term = cw_ref[j:j + 1, lanes] * glu_ref[m, pl.ds(sub * q, rb + sub), lanes]
                        part = term if part is None else part + term
                shifted = part[ph:ph + rb]
                acc = shifted if acc is None else acc + shifted
            dw_ref[out_rows, lanes] = acc + cb_ref[:, lanes]

    ln_rows = 128

    def ln_body(r, c):
        sl = pl.ds(pl.multiple_of(r * ln_rows, ln_rows), ln_rows)
        x = dw_ref[sl, :]
        mu = jnp.mean(x, axis=-1, keepdims=True)
        xc = x - mu
        var = jnp.mean(xc * xc, axis=-1, keepdims=True)
        y = xc * lax.rsqrt(var + EPS) * lg_ref[...] + lb_ref[...]
        act_ref[slot, sl, :] = (y * _sigmoid(y)).astype(BF16)
        return c

    lax.fori_loop(0, tt // ln_rows, ln_body, 0)


def _conv_module(P, conv_w, conv_b, ln_g, ln_b, w_pw2, B, S, *, tt=512):
    T = B * S
    nt = S // tt
    n_tiles = B * nt
    hb = tt // HALO
    n_halo_blocks = T // HALO
    ca, cb = OFF_PWA // D_MODEL, OFF_PWB // D_MODEL
    tile = lambda s: jnp.minimum(s, n_tiles - 1)
    prev = lambda s: jnp.maximum(tile(s) * hb - 1, 0)
    nxt = lambda s: jnp.minimum((tile(s) + 1) * hb, n_halo_blocks - 1)
    vec = lambda n: pl.BlockSpec((n, D_MODEL), lambda s: (0, 0))
    rb = 64
    return pl.pallas_call(
        functools.partial(_conv_kernel, rb=rb, lb=128, nt=nt),
        grid=(n_tiles + 1,),
        in_specs=[
            pl.BlockSpec((tt, D_MODEL), lambda s: (tile(s), ca)),
            pl.BlockSpec((tt, D_MODEL), lambda s: (tile(s), cb)),
            pl.BlockSpec((HALO, D_MODEL), lambda s: (prev(s), ca)),
            pl.BlockSpec((HALO, D_MODEL), lambda s: (prev(s), cb)),
            pl.BlockSpec((HALO, D_MODEL), lambda s: (nxt(s), ca)),
            pl.BlockSpec((HALO, D_MODEL), lambda s: (nxt(s), cb)),
            vec(CONV_WIDTH), vec(1), vec(1), vec(1),
            pl.BlockSpec((D_MODEL, D_MODEL), lambda s: (0, 0), pipeline_mode=pl.Buffered(1)),
        ],
        out_specs=pl.BlockSpec((tt, D_MODEL), lambda s: (jnp.maximum(s - 1, 0), 0)),
        out_shape=jax.ShapeDtypeStruct((T, D_MODEL), BF16),
        scratch_shapes=[
            pltpu.VMEM((tt // rb, rb + 2 * HALO, D_MODEL), F32),
            pltpu.VMEM((tt, D_MODEL), F32),
            pltpu.VMEM((2, tt, D_MODEL), BF16),
        ],
        compiler_params=_cparams(("arbitrary",), 52),
        name="conv_module",
    )(P, P, P, P, P, P, conv_w, conv_b, ln_g, ln_b, w_pw2)


def _merge_kernel(of_ref, ob_ref, g_ref, ga_ref, gb_ref, cb_ref, x_ref, gn_ref, gpost_ref, wo_ref,
                  h_ref, m_ref, *, rows, norm_rows):
    s = pl.program_id(0)
    slot = s % 2
    tt = x_ref.shape[0]
    n_blk = tt // rows
    n_col = D_MODEL // n_blk

    @pl.when(s == 0)
    def _():
        m_ref[1] = jnp.zeros(m_ref.shape[1:], m_ref.dtype)

    for r in range(n_blk):
        cols = slice(r * n_col, (r + 1) * n_col)
        h_ref[:, cols] = jnp.dot(m_ref[1 - slot], wo_ref[:, cols], preferred_element_type=F32)
        sl = slice(r * rows, (r + 1) * rows)
        o = of_ref[sl, :].astype(F32) + ob_ref[sl, :].astype(F32)
        heads = []
        for hd in range(N_HEADS):
            oh = o[:, hd * HEAD_DV:(hd + 1) * HEAD_DV]
            ms = jnp.mean(oh * oh, axis=-1, keepdims=True)
            heads.append(oh * lax.rsqrt(ms + EPS))
        on = jnp.concatenate(heads, axis=-1) * gn_ref[...]
        g = g_ref[sl, :].astype(F32)
        out_a = on * (g * _sigmoid(g))
        merged = (_sigmoid(ga_ref[sl, :].astype(F32)) * out_a
                  + _sigmoid(gb_ref[sl, :].astype(F32)) * cb_ref[sl, :].astype(F32))
        m_ref[slot, sl, :] = merged.astype(BF16)

    def norm_body(r, c):
        sl = pl.ds(pl.multiple_of(r * norm_rows, norm_rows), norm_rows)
        y = h_ref[sl, :]
        ms = jnp.mean(y * y, axis=-1, keepdims=True)
        h_ref[sl, :] = x_ref[sl, :] + (y * lax.rsqrt(ms + EPS)) * gpost_ref[...]
        return c

    lax.fori_loop(0, tt // norm_rows, norm_body, 0)


def _merge(o_f, o_b, P, out_b, x2d, gla_norm, g_post, w_out, *, tt=512):
    T = x2d.shape[0]
    n_tiles = T // tt
    cur = lambda c: pl.BlockSpec((tt, D_MODEL), lambda s: (jnp.minimum(s, n_tiles - 1), c))
    lag = pl.BlockSpec((tt, D_MODEL), lambda s: (jnp.maximum(s - 1, 0), 0))
    vec = pl.BlockSpec((1, D_MODEL), lambda s: (0, 0))
    return pl.pallas_call(
        functools.partial(_merge_kernel, rows=64, norm_rows=64),
        grid=(n_tiles + 1,),
        in_specs=[
            cur(0), cur(0),
            cur(OFF_G // D_MODEL), cur(OFF_GA // D_MODEL), cur(OFF_GB // D_MODEL),
            cur(0), lag, vec, vec,
            pl.BlockSpec((D_MODEL, D_MODEL), lambda s: (0, 0), pipeline_mode=pl.Buffered(1)),
        ],
        out_specs=lag,
        out_shape=jax.ShapeDtypeStruct((T, D_MODEL), F32),
        scratch_shapes=[pltpu.VMEM((2, tt, D_MODEL), BF16)],
        compiler_params=_cparams(("arbitrary",), 56),
        name="merge_outproj",
    )(o_f, o_b, P, P, P, out_b, x2d, gla_norm, g_post, w_out)


def _ffn_kernel(h_ref, gpre_ref, gpost_ref, w1_ref, w2_ref, y_ref, u_ref, *, rows):
    acc_ref = y_ref
    f = pl.program_id(1)
    tm = h_ref.shape[0]

    @pl.when(f == 0)
    def _():
        def body(r, c):
            sl = pl.ds(pl.multiple_of(r * rows, rows), rows)
            x = h_ref[sl, :]
            ms = jnp.mean(x * x, axis=-1, keepdims=True)
            u_ref[sl, :] = ((x * lax.rsqrt(ms + EPS)) * gpre_ref[...]).astype(BF16)
            return c
        lax.fori_loop(0, tm // rows, body, 0)
        acc_ref[...] = jnp.zeros_like(acc_ref)

    a = jnp.dot(u_ref[...], w1_ref[...], preferred_element_type=F32)
    a = jnp.maximum(a, 0.0)
    a = (a * a).astype(BF16)
    acc_ref[...] += jnp.dot(a, w2_ref[...], preferred_element_type=F32)

    @pl.when(f == pl.num_programs(1) - 1)
    def _():
        def body(r, c):
            sl = pl.ds(pl.multiple_of(r * rows, rows), rows)
            y = acc_ref[sl, :]
            ms = jnp.mean(y * y, axis=-1, keepdims=True)
            y_ref[sl, :] = h_ref[sl, :] + (y * lax.rsqrt(ms + EPS)) * gpost_ref[...]
            return c
        lax.fori_loop(0, tm // rows, body, 0)


def _ffn(h2d, g_pre, g_post, w1, w2, *, tm=1024, tf=1024):
    T = h2d.shape[0]
    vec = pl.BlockSpec((1, D_MODEL), lambda i, f: (0, 0))
    return pl.pallas_call(
        functools.partial(_ffn_kernel, rows=128),
        grid=(T // tm, D_FF // tf),
        in_specs=[
            pl.BlockSpec((tm, D_MODEL), lambda i, f: (i, 0)),
            vec, vec,
            pl.BlockSpec((D_MODEL, tf), lambda i, f: (0, f)),
            pl.BlockSpec((tf, D_MODEL), lambda i, f: (f, 0)),
        ],
        out_specs=pl.BlockSpec((tm, D_MODEL), lambda i, f: (i, 0)),
        out_shape=jax.ShapeDtypeStruct((T, D_MODEL), F32),
        scratch_shapes=[pltpu.VMEM((tm, D_MODEL), BF16)],
        compiler_params=_cparams(("parallel", "arbitrary"), 58),
        name="ffn",
    )(h2d, g_pre, g_post, w1, w2)


def _encoder_layer(x, wts):
    B, S, _ = x.shape
    x2d = x.reshape(B * S, D_MODEL)
    P, Z = _inproj(x2d, wts["g_mix_pre"], wts["w_main"], wts["w_z"])
    o_f, o_b = _gla(P, Z, wts["wa_f"], wts["ba_f"], wts["wa_b"], wts["ba_b"], B, S)
    out_b = _conv_module(P, wts["conv_w"], wts["conv_b"], wts["ln_g"], wts["ln_b"], wts["w_pw2"], B, S)
    h = _merge(o_f, o_b, P, out_b, x2d, wts["gla_norm"], wts["g_mix_post"], wts["w_out"])
    y = _ffn(h, wts["g_ffn_pre"], wts["g_ffn_post"], wts["w_ff1"], wts["w_ff2"])
    return y.reshape(B, S, D_MODEL)


def kernel(x_prompt, x_sample, norm_mix_pre, norm_mix_post, norm_ffn_pre, norm_ffn_post, w_in, w_a_fwd, b_a_fwd, w_a_bwd, b_a_bwd, gla_norm, conv_w, conv_b, conv_ln_g, conv_ln_b, w_pw2, w_out, w_ff1, w_ff2):
    z0 = 2 * D_K + 2 * D_V
    z1 = z0 + 2 * GATE_RANK
    row = lambda v: v.reshape(1, -1).astype(F32)
    w_z = jnp.zeros((D_MODEL, Z_COLS), F32).at[:, :2 * GATE_RANK].set(w_in[:, z0:z1])
    wa_f = jnp.zeros((Z_COLS, D_K), F32).at[:GATE_RANK].set(w_a_fwd)
    wa_b = jnp.zeros((Z_COLS, D_K), F32).at[GATE_RANK:2 * GATE_RANK].set(w_a_bwd)
    wts = {
        "g_mix_pre": row(norm_mix_pre), "g_mix_post": row(norm_mix_post),
        "g_ffn_pre": row(norm_ffn_pre), "g_ffn_post": row(norm_ffn_post),
        "w_main": w_in.T.astype(BF16),
        "w_z": w_z.astype(BF16),
        "wa_f": wa_f.astype(BF16), "ba_f": row(b_a_fwd),
        "wa_b": wa_b.astype(BF16), "ba_b": row(b_a_bwd),
        "gla_norm": row(gla_norm),
        "conv_w": conv_w, "conv_b": row(conv_b),
        "ln_g": row(conv_ln_g), "ln_b": row(conv_ln_b),
        "w_pw2": w_pw2.astype(BF16), "w_out": w_out.astype(BF16),
        "w_ff1": w_ff1.astype(BF16), "w_ff2": w_ff2.astype(BF16),
    }
    return (_encoder_layer(x_prompt, wts), _encoder_layer(x_sample, wts))
```
